```python
import math
import jax, jax.numpy as jnp
from jax import lax
import numpy as np

D_MODEL = 1024
BATCH = 16
SEQ = 2048
DEPTH = 2

GRID_W = 64
CTX_LEN = 256
HEAD_DIM = 64
MIX_WIDTH = D_MODEL
FOURIER_GROUPS = 4
FOURIER_WIDTH = MIX_WIDTH // 4
FOURIER_GROUP_DIM = FOURIER_WIDTH // FOURIER_GROUPS
GQA_HEADS = (MIX_WIDTH // 2) // HEAD_DIM
GQA_KV_HEADS = GQA_HEADS // 4
NA_HEADS = (MIX_WIDTH // 4) // HEAD_DIM
NA_WIN_H = 8
NA_WIN_W = 16
Q_BLOCK = 128
ROPE_THETA = 10000.0
ROPE_PAIRS_PER_AXIS = HEAD_DIM // 4
D_FF = 256 * math.ceil(8 * D_MODEL / 3 / 256)
N_BRANCHES = 3
N_MOD = 9
EPS = 1e-6
GQA_Q_WIDTH = GQA_HEADS * HEAD_DIM
GQA_KV_WIDTH = GQA_KV_HEADS * HEAD_DIM
NA_WIDTH = NA_HEADS * HEAD_DIM
PROJ_SPLITS = tuple(int(v) for v in np.cumsum([FOURIER_WIDTH, GQA_Q_WIDTH, GQA_KV_WIDTH, GQA_KV_WIDTH, NA_WIDTH, NA_WIDTH, NA_WIDTH]))
IN_PROJ_WIDTH = PROJ_SPLITS[-1] + N_BRANCHES * D_MODEL

kernel_name = "hybrid_gated_mixer_dit_block"


def rms_norm(x, gain):
    xf = x.astype(jnp.float32)
    y = xf * lax.rsqrt(jnp.mean(xf * xf, axis=-1, keepdims=True) + EPS)
    return (y * gain.astype(jnp.float32)).astype(x.dtype)


def modulate(u, shift, scale):
    return u * (1 + scale) + shift


def swiglu(u, w_in, w_out):
    a, g = jnp.split(u @ w_in, 2, axis=-1)
    return (jax.nn.silu(g) * a) @ w_out


def heads(t, n):
    return t.reshape(t.shape[0], t.shape[1], n, HEAD_DIM)


def axial_rope_tables(seq_len):
    t = jnp.arange(seq_len)
    row = (t // GRID_W).astype(jnp.float32)
    col = (t % GRID_W).astype(jnp.float32)
    freqs = ROPE_THETA ** (-jnp.arange(ROPE_PAIRS_PER_AXIS, dtype=jnp.float32) / ROPE_PAIRS_PER_AXIS)
    ang = jnp.concatenate([row[:, None] * freqs, col[:, None] * freqs], axis=-1)
    return jnp.cos(ang), jnp.sin(ang)


def apply_rope(x, cos, sin):
    xf = x.astype(jnp.float32).reshape(*x.shape[:-1], HEAD_DIM // 2, 2)
    x0, x1 = xf[..., 0], xf[..., 1]
    c, s = cos[None, :, None, :], sin[None, :, None, :]
    out = jnp.stack([x0 * c - x1 * s, x0 * s + x1 * c], axis=-1)
    return out.reshape(x.shape).astype(x.dtype)


def fourier_mix(u):
    b, t, _ = u.shape
    ug = u.astype(jnp.float32).reshape(b, t, FOURIER_GROUPS, FOURIER_GROUP_DIM)
    y = jnp.fft.fft2(ug, axes=(1, 3), norm="ortho").real
    return y.reshape(b, t, FOURIER_WIDTH).astype(u.dtype)


def gqa_attend(q, k, v):
    b, t, h, hd = q.shape
    kvh = k.shape[2]
    qg = q.reshape(b, t, kvh, h // kvh, hd)
    s = jnp.einsum("btkgd,blkd->bkgtl", qg, k).astype(jnp.float32) * (hd ** -0.5)
    p = jax.nn.softmax(s, axis=-1).astype(v.dtype)
    o = jnp.einsum("bkgtl,blkd->btkgd", p, v)
    return o.reshape(b, t, h * hd)


def gqa_latent(q, k_all, v_all):
    b, s, h, hd = q.shape
    qb = jnp.moveaxis(q.reshape(b, s // Q_BLOCK, Q_BLOCK, h, hd), 1, 0)
    out = lax.map(lambda blk: gqa_attend(blk, k_all, v_all), qb)
    return jnp.moveaxis(out, 0, 1).reshape(b, s, h * hd)


def neighbourhood_attention(q, k, v, k_ctx, v_ctx, rpb):
    b, s, nh, hd = q.shape
    rows = s // GRID_W
    kh = min(NA_WIN_H, rows)
    qg = jnp.moveaxis(q.reshape(b, rows, GRID_W, nh, hd), 1, 0)
    kg = k.reshape(b, rows, GRID_W, nh, hd)
    vg = v.reshape(b, rows, GRID_W, nh, hd)
    col = jnp.arange(GRID_W)
    cs = jnp.clip(col - NA_WIN_W // 2, 0, GRID_W - NA_WIN_W)
    band = (col[None, :] >= cs[:, None]) & (col[None, :] < cs[:, None] + NA_WIN_W)
    dc = jnp.clip(col[None, :] - col[:, None], -(NA_WIN_W - 1), NA_WIN_W - 1) + NA_WIN_W - 1
    scale = hd ** -0.5

    def row_block(args):
        q_row, r = args
        rs = jnp.clip(r - kh // 2, 0, rows - kh)
        k_blk = lax.dynamic_slice_in_dim(kg, rs, kh, axis=1)
        v_blk = lax.dynamic_slice_in_dim(vg, rs, kh, axis=1)
        dr = rs + jnp.arange(kh) - r + NA_WIN_H - 1
        bias = rpb[:, dr[:, None, None], dc[None, :, :]]
        bias = jnp.transpose(bias, (0, 2, 1, 3)).astype(jnp.float32)
        s_loc = jnp.einsum("bqhd,bikhd->bhqik", q_row, k_blk).astype(jnp.float32) * scale + bias[None]
        s_loc = jnp.where(band[:, None, :], s_loc, -jnp.inf).reshape(b, nh, GRID_W, kh * GRID_W)
        s_ctx = jnp.einsum("bqhd,bjhd->bhqj", q_row, k_ctx).astype(jnp.float32) * scale
        p = jax.nn.softmax(jnp.concatenate([s_loc, s_ctx], axis=-1), axis=-1).astype(v.dtype)
        p_loc = p[..., : kh * GRID_W].reshape(b, nh, GRID_W, kh, GRID_W)
        p_ctx = p[..., kh * GRID_W:]
        return (jnp.einsum("bhqik,bikhd->bqhd", p_loc, v_blk)
                + jnp.einsum("bhqj,bjhd->bqhd", p_ctx, v_ctx))

    out = lax.map(row_block, (qg, jnp.arange(rows)))
    return jnp.moveaxis(out, 0, 1).reshape(b, s, nh * hd)


def gated_merge(gate_logits, y_f, y_g, y_n):
    g_f, g_g, g_n = jnp.split(jax.nn.sigmoid(gate_logits), N_BRANCHES, axis=-1)
    return g_f * y_f + g_g * y_g + g_n * y_n


def token_mixer(ux, uh, w_in, q_norm, k_norm, na_rpb, w_fourier, w_gqa_out, w_na_out, w_o, cos, sin, ctx_out):
    fx, gqx, gkx, gvx, nqx, nkx, nvx, gatex = jnp.split(ux @ w_in, PROJ_SPLITS, axis=-1)
    fh, gqh, gkh, gvh, nqh, nkh, nvh, gateh = jnp.split(uh @ w_in, PROJ_SPLITS, axis=-1)
    gk_h = rms_norm(heads(gkh, GQA_KV_HEADS), k_norm)
    gv_h = heads(gvh, GQA_KV_HEADS)
    nk_h = heads(nkh, NA_HEADS)
    nv_h = heads(nvh, NA_HEADS)
    y_f = fourier_mix(fx) @ w_fourier
    q = apply_rope(rms_norm(heads(gqx, GQA_HEADS), q_norm), cos, sin)
    k = apply_rope(rms_norm(heads(gkx, GQA_KV_HEADS), k_norm), cos, sin)
    v = heads(gvx, GQA_KV_HEADS)
    y_g = gqa_latent(q, jnp.concatenate([gk_h, k], axis=1), jnp.concatenate([gv_h, v], axis=1)) @ w_gqa_out
    y_n = neighbourhood_attention(heads(nqx, NA_HEADS), heads(nkx, NA_HEADS), heads(nvx, NA_HEADS), nk_h, nv_h, na_rpb) @ w_na_out
    yx = gated_merge(gatex, y_f, y_g, y_n) @ w_o
    if not ctx_out:
        return yx, None
    y_fh = fourier_mix(fh) @ w_fourier
    y_gh = gqa_attend(rms_norm(heads(gqh, GQA_HEADS), q_norm), gk_h, gv_h) @ w_gqa_out
    y_nh = gqa_attend(heads(nqh, NA_HEADS), nk_h, nv_h) @ w_na_out
    yh = gated_merge(gateh, y_fh, y_gh, y_nh) @ w_o
    return yx, yh


def setup_inputs(seed: int = 0) -> dict:
    key = jax.random.key(seed)
    ks = jax.random.split(key, 24)
    f32 = jnp.float32
    L, D = DEPTH, D_MODEL

    def w(k, shape, fan_in, gain=1.0):
        return (gain * fan_in ** -0.5) * jax.random.normal(k, shape, f32)

    def g(k, shape):
        return 1.0 + 0.02 * jax.random.normal(k, shape, f32)

    return {
        "x": jax.random.normal(ks[0], (BATCH, SEQ, D), f32),
        "c": jax.random.normal(ks[1], (BATCH, D), f32),
        "ctx": jax.random.normal(ks[2], (BATCH, CTX_LEN, D), f32),
        "c_ctx": jax.random.normal(ks[3], (D,), f32),
        "mod_w": w(ks[4], (L, D, N_MOD * D), D, 0.5),
        "mod_b": 0.01 * jax.random.normal(ks[5], (L, N_MOD * D), f32),
        "norm_ffn1": g(ks[6], (L, D)),
        "ffn1_w_in": w(ks[7], (L, D, 2 * D_FF), D),
        "ffn1_w_out": w(ks[8], (L, D_FF, D), D_FF),
        "norm_mix": g(ks[9], (L, D)),
        "w_in": w(ks[10], (L, D, IN_PROJ_WIDTH), D),
        "q_norm": g(ks[11], (L, HEAD_DIM)),
        "k_norm": g(ks[12], (L, HEAD_DIM)),
        "na_rpb": 0.1 * jax.random.normal(ks[13], (L, NA_HEADS, 2 * NA_WIN_H - 1, 2 * NA_WIN_W - 1), f32),
        "w_fourier": w(ks[14], (L, FOURIER_WIDTH, D), FOURIER_WIDTH),
        "w_gqa_out": w(ks[15], (L, GQA_Q_WIDTH, D), GQA_Q_WIDTH),
        "w_na_out": w(ks[16], (L, NA_WIDTH, D), NA_WIDTH),
        "w_o": w(ks[17], (L, D, D), D),
        "norm_ffn2": g(ks[18], (L, D)),
        "ffn2_w_in": w(ks[19], (L, D, 2 * D_FF), D),
        "ffn2_w_out": w(ks[20], (L, D_FF, D), D_FF),
        "final_norm": g(ks[21], (D,)),
    }


def reference(x, c, ctx, c_ctx, mod_w, mod_b, norm_ffn1, ffn1_w_in, ffn1_w_out, norm_mix, w_in, q_norm, k_norm, na_rpb, w_fourier, w_gqa_out, w_na_out, w_o, norm_ffn2, ffn2_w_in, ffn2_w_out, final_norm):
    cos, sin = axial_rope_tables(x.shape[1])
    h = ctx
    for l in range(DEPTH):
        last = l == DEPTH - 1
        mx = jnp.split((jax.nn.silu(c) @ mod_w[l] + mod_b[l])[:, None, :], N_MOD, axis=-1)
        mh = jnp.split((jax.nn.silu(c_ctx) @ mod_w[l] + mod_b[l])[None, None, :], N_MOD, axis=-1)
        x = x + 0.5 * mx[2] * swiglu(modulate(rms_norm(x, norm_ffn1[l]), mx[0], mx[1]), ffn1_w_in[l], ffn1_w_out[l])
        h = h + 0.5 * mh[2] * swiglu(modulate(rms_norm(h, norm_ffn1[l]), mh[0], mh[1]), ffn1_w_in[l], ffn1_w_out[l])
        ux = modulate(rms_norm(x, norm_mix[l]), mx[3], mx[4])
        uh = modulate(rms_norm(h, norm_mix[l]), mh[3], mh[4])
        yx, yh = token_mixer(ux, uh, w_in[l], q_norm[l], k_norm[l], na_rpb[l], w_fourier[l], w_gqa_out[l], w_na_out[l], w_o[l], cos, sin, not last)
        x = x + mx[5] * yx
        x = x + 0.5 * mx[8] * swiglu(modulate(rms_norm(x, norm_ffn2[l]), mx[6], mx[7]), ffn2_w_in[l], ffn2_w_out[l])
        if not last:
            h = h + mh[5] * yh
            h = h + 0.5 * mh[8] * swiglu(modulate(rms_norm(h, norm_ffn2[l]), mh[6], mh[7]), ffn2_w_in[l], ffn2_w_out[l])
    return rms_norm(x, final_norm)
```

```python
import functools
import math

import numpy as np
import jax
import jax.numpy as jnp
from jax import lax
from jax.experimental import pallas as pl
from jax.experimental.pallas import tpu as pltpu

HEAD_DIM = 64
GRID_W = 64
FOURIER_GROUPS = 4
NA_WIN_H = 8
NA_WIN_W = 16
ROPE_THETA = 10000.0
N_MOD = 9
EPS = 1e-6
KV_GROUP = 4

LANES = 128
VMEM_PHYSICAL_BYTES = 64 * 1024 * 1024

NA_Q_ROWS = 4
NA_K_ROWS = NA_Q_ROWS + NA_WIN_H
NEG_BIG = -1e30

F32 = jnp.float32
BF16 = jnp.bfloat16


def _dot(a, b):
    return jnp.dot(a, b, preferred_element_type=F32)


def _dot_nt(a, b):
    return lax.dot_general(a, b, (((1,), (1,)), ((), ())), preferred_element_type=F32)


def _params(sem, vmem_bytes):
    return pltpu.CompilerParams(dimension_semantics=sem,
                                vmem_limit_bytes=int(min(vmem_bytes, VMEM_PHYSICAL_BYTES - (4 << 20))))


def _resident(shape, index_map):
    return pl.BlockSpec(shape, index_map, pipeline_mode=pl.Buffered(1))


def _norm_mod(x, gain, shift, scale):
    y = x * lax.rsqrt(jnp.mean(x * x, axis=-1, keepdims=True) + EPS) * gain
    return y * (1.0 + scale) + shift


def _mod_kernel(c_ref, w_ref, b_ref, o_ref):
    c = c_ref[...]
    s = (c * jax.nn.sigmoid(c)).astype(BF16)
    o_ref[0] = _dot(s, w_ref[0].astype(BF16)) + b_ref[0]


def _modulation(cc, mod_w, mod_b, tn=1024):
    depth, d, n = mod_w.shape
    rows = cc.shape[0]
    return pl.pallas_call(
        _mod_kernel,
        grid=(depth, n // tn),
        in_specs=[pl.BlockSpec((rows, d), lambda l, j: (0, 0)),
                  pl.BlockSpec((1, d, tn), lambda l, j: (l, 0, j)),
                  pl.BlockSpec((1, 1, tn), lambda l, j: (l, 0, j))],
        out_specs=pl.BlockSpec((1, rows, tn), lambda l, j: (l, 0, j)),
        out_shape=jax.ShapeDtypeStruct((depth, rows, n), F32),
        compiler_params=_params(("parallel", "parallel"), 2 * (d * tn * 4) + 8 * d * tn),
        name="modulation",
    )(cc, mod_w, mod_b.reshape(depth, 1, n))


def _ffn_kernel(*refs, d_ff, tf, final):
    if final:
        x_ref, sh_ref, sc_ref, gt_ref, g_ref, win_ref, wout_ref, fg_ref, o_ref, u_scr, h_scr = refs
    else:
        x_ref, sh_ref, sc_ref, gt_ref, g_ref, win_ref, wout_ref, o_ref, u_scr, h_scr = refs
    x = x_ref[0]
    u_scr[...] = _norm_mod(x, g_ref[...], sh_ref[0], sc_ref[0]).astype(BF16)
    for j in range(d_ff // tf):
        u = u_scr[...]
        a = _dot(u, win_ref[:, j * tf:(j + 1) * tf])
        g = _dot(u, win_ref[:, d_ff + j * tf:d_ff + (j + 1) * tf])
        h_scr[:, j * tf:(j + 1) * tf] = (g * jax.nn.sigmoid(g) * a).astype(BF16)
    y = x + (0.5 * gt_ref[0]) * _dot(h_scr[...], wout_ref[...])
    if final:
        y = y * lax.rsqrt(jnp.mean(y * y, axis=-1, keepdims=True) + EPS) * fg_ref[...]
    o_ref[0] = y


def _ffn(x, mod3, row_fn, k0, gain, w_in, w_out, *, tm, final_gain=None, tf=256):
    nb, t, d = x.shape
    d_ff = w_out.shape[0]
    final = final_gain is not None
    row = lambda k: pl.BlockSpec((1, 1, d), lambda b, i: (row_fn(b, k), 0, 0))
    vec = pl.BlockSpec((1, d), lambda b, i: (0, 0))
    in_specs = [pl.BlockSpec((1, tm, d), lambda b, i: (b, i, 0)), row(k0), row(k0 + 1), row(k0 + 2), vec,
                _resident((d, 2 * d_ff), lambda b, i: (0, 0)),
                _resident((d_ff, d), lambda b, i: (0, 0))]
    args = [x, mod3, mod3, mod3, gain.reshape(1, d), w_in, w_out]
    if final:
        in_specs.append(vec)
        args.append(final_gain.reshape(1, d))
    vmem = (3 * d * d_ff * 2
            + 4 * tm * d * 4
            + tm * d * 2 + tm * d_ff * 2
            + 2 * tm * d * 4 + 4 * tm * tf * 4)
    return pl.pallas_call(
        functools.partial(_ffn_kernel, d_ff=d_ff, tf=tf, final=final),
        grid=(nb, t // tm),
        in_specs=in_specs,
        out_specs=pl.BlockSpec((1, tm, d), lambda b, i: (b, i, 0)),
        out_shape=jax.ShapeDtypeStruct(x.shape, F32),
        scratch_shapes=[pltpu.VMEM((tm, d), BF16), pltpu.VMEM((tm, d_ff), BF16)],
        compiler_params=_params(("parallel", "parallel"), vmem + (8 << 20)),
        name="ffn",
    )(*args)


def _head_norm(x, gain, ones):
    x2 = x * x
    hi = x2.astype(BF16)
    lo = (x2 - hi.astype(F32)).astype(BF16)
    ss = _dot(hi, ones) + _dot(lo, ones)
    return x * lax.rsqrt(ss * (1.0 / HEAD_DIM) + EPS) * gain


def _rope(y, cos, ssin):
    even = lax.broadcasted_iota(jnp.int32, (1, LANES), 1) % 2 == 0
    cols = []
    for c in range(y.shape[1] // LANES):
        yc = y[:, c * LANES:(c + 1) * LANES]
        partner = jnp.where(even, pltpu.roll(yc, LANES - 1, 1), pltpu.roll(yc, 1, 1))
        cols.append(yc * cos + partner * ssin)
    return cols[0] if len(cols) == 1 else jnp.concatenate(cols, axis=1)


def _inproj_kernel(*refs, rope, splits, d):
    if rope:
        (x_ref, sh_ref, sc_ref, g_ref, w_ref, qg_ref, kg_ref, ones_ref, cos_ref, sin_ref,
         f_ref, q_ref, k_ref, v_ref, nq_ref, nk_ref, nv_ref, sg_ref, u_scr) = refs
    else:
        (x_ref, sh_ref, sc_ref, g_ref, w_ref, qg_ref, kg_ref, ones_ref,
         f_ref, q_ref, k_ref, v_ref, nq_ref, nk_ref, nv_ref, sg_ref, u_scr) = refs
    m = u_scr.shape[0]
    x = x_ref[...].reshape(m, d)
    u_scr[...] = _norm_mod(x, g_ref[...], sh_ref[0], sc_ref[0]).astype(BF16)
    s_f, s_q, s_k, s_v, s_nq, s_nk, s_nv = splits
    r = _dot(u_scr[...], w_ref[:, :s_nv])
    kw = s_k - s_q
    q = _head_norm(r[:, s_f:s_q], qg_ref[...], ones_ref[...])
    k = _head_norm(r[:, s_q:s_k], kg_ref[...], ones_ref[:kw, :kw])
    if rope:
        q = _rope(q, cos_ref[...], sin_ref[...])
        k = _rope(k, cos_ref[...], sin_ref[...])
    scale = HEAD_DIM ** -0.5
    put = lambda ref, val: ref.__setitem__(Ellipsis, val.astype(BF16).reshape(ref.shape))
    put(f_ref, r[:, :s_f])
    put(q_ref, q * scale)
    put(k_ref, k)
    put(v_ref, r[:, s_k:s_v])
    put(nq_ref, r[:, s_v:s_nq] * scale)
    put(nk_ref, r[:, s_nq:s_nk])
    put(nv_ref, r[:, s_nk:s_nv])
    for c in range(3):
        g = _dot(u_scr[...], w_ref[:, s_nv + c * d:s_nv + (c + 1) * d])
        sg_ref[..., c * d:(c + 1) * d] = jax.nn.sigmoid(g).astype(BF16).reshape(sg_ref.shape[:-1] + (d,))


def _inproj(x, mod3, row_fn, gain, w, qg, kg, ones, tables, *, nb, tm):
    b_, t, d = x.shape
    n_tot = w.shape[1]
    fw, qw, kw, naw = d // 4, d // 2, d // 8, d // 4
    widths = [fw, qw, kw, kw, naw, naw, naw]
    splits = tuple(int(v) for v in np.cumsum(widths))
    rope = tables is not None
    row = lambda k: pl.BlockSpec((1, 1, d), lambda b, i: (row_fn(b, k), 0, 0))
    const = lambda shape: pl.BlockSpec(shape, lambda b, i: (0,) * len(shape))
    tok = lambda wd: pl.BlockSpec((nb, tm, wd), lambda b, i: (b, i, 0))
    in_specs = [tok(d), row(3), row(4), const((1, d)), _resident((d, n_tot), lambda b, i: (0, 0)),
                const((1, qw)), const((1, kw)), const((qw, qw))]
    args = [x, mod3, mod3, gain.reshape(1, d), w, qg, kg, ones]
    if rope:
        in_specs += [pl.BlockSpec((tm, LANES), lambda b, i: (i, 0))] * 2
        args += list(tables)
    out_widths = widths + [3 * d]
    m = nb * tm
    vmem = (d * n_tot * 2 + 2 * m * d * 4 + m * d * 2 + 2 * m * n_tot * 2
            + m * splits[-1] * 4 + 6 * m * qw * 4 + 2 * m * d * 4)
    return pl.pallas_call(
        functools.partial(_inproj_kernel, rope=rope, splits=splits, d=d),
        grid=(b_ // nb, t // tm),
        in_specs=in_specs,
        out_specs=[tok(wd) for wd in out_widths],
        out_shape=[jax.ShapeDtypeStruct((b_, t, wd), BF16) for wd in out_widths],
        scratch_shapes=[pltpu.VMEM((m, d), BF16)],
        compiler_params=_params(("parallel", "parallel"), vmem + (8 << 20)),
        name="inproj",
    )(*args)


def _fourier_kernel(x_ref, cs_ref, ct_ref, st_ref, o_ref):
    w = x_ref.shape[2]
    ab = _dot(x_ref[0], cs_ref[...]).astype(BF16)
    y = _dot(ct_ref[...], ab[:, :w]) - _dot(st_ref[...], ab[:, w:])
    o_ref[0] = y.astype(BF16)


def _dft_tables(t, w):
    def cs(n):
        i = jnp.arange(n, dtype=jnp.int32)
        ang = ((i[:, None] * i[None, :]) % n).astype(F32) * (2.0 * math.pi / n)
        return jnp.cos(ang) * (1.0 / math.sqrt(n)), jnp.sin(ang) * (1.0 / math.sqrt(n))
    ct, st = cs(t)
    cg, sg = cs(w // FOURIER_GROUPS)
    eye = jnp.eye(FOURIER_GROUPS, dtype=F32)
    cs_c = jnp.concatenate([jnp.kron(eye, cg), jnp.kron(eye, sg)], axis=1)
    return cs_c.astype(BF16), ct.astype(BF16), st.astype(BF16)


def _fourier(f, tables):
    b_, t, w = f.shape
    cs_c, ct, st = tables
    return pl.pallas_call(
        _fourier_kernel,
        grid=(b_,),
        in_specs=[pl.BlockSpec((1, t, w), lambda b: (b, 0, 0)),
                  _resident((w, 2 * w), lambda b: (0, 0)),
                  _resident((t, t), lambda b: (0, 0)),
                  _resident((t, t), lambda b: (0, 0))],
        out_specs=pl.BlockSpec((1, t, w), lambda b: (b, 0, 0)),
        out_shape=jax.ShapeDtypeStruct(f.shape, BF16),
        compiler_params=_params(("parallel",), 2 * t * t * 2 + 4 * t * w * 2 + 4 * t * w * 4 + (8 << 20)),
        name="fourier",
    )(f, cs_c, ct, st)


def _lane_lo():
    return lax.broadcasted_iota(jnp.int32, (1, LANES), 1) < HEAD_DIM


def _stack_heads(qc, lo):
    zero = jnp.zeros_like(qc)
    return jnp.concatenate([jnp.where(lo, qc, zero), jnp.where(lo, zero, qc)], axis=0)


def _unstack_heads(o, lo):
    t = o.shape[0] // 2
    return jnp.where(lo, o[:t], o[t:])


def _attend(q2, blocks):
    scores = []
    for k, _, bias in blocks:
        s = _dot_nt(q2, k)
        scores.append(s if bias is None else s + bias)
    m = functools.reduce(jnp.maximum, [jnp.max(s, axis=-1, keepdims=True) for s in scores])
    probs = [jnp.exp(s - m) for s in scores]
    l = functools.reduce(jnp.add, [jnp.sum(p, axis=-1, keepdims=True) for p in probs])
    o = functools.reduce(jnp.add, [_dot(p.astype(BF16), v) for p, (_, v, _) in zip(probs, blocks)])
    return o / l


def _gqa_kernel(*refs):
    q_ref, o_ref = refs[0], refs[-1]
    kv_refs = refs[1:-1]
    lo = _lane_lo()
    for c in range(q_ref.shape[2] // LANES):
        cols = slice(c * LANES, (c + 1) * LANES)
        q2 = _stack_heads(q_ref[0, :, cols], lo)
        blocks = []
        for k_ref, v_ref in zip(kv_refs[0::2], kv_refs[1::2]):
            kc = cols if k_ref.shape[2] > LANES else slice(0, LANES)
            blocks.append((k_ref[0, :, kc], v_ref[0, :, kc], None))
        o_ref[0, :, cols] = _unstack_heads(_attend(q2, blocks), lo).astype(BF16)


def _gqa(q, kvs, *, tq):
    b_, t, w = q.shape
    in_specs = [pl.BlockSpec((1, tq, w), lambda b, i: (b, i, 0))]
    args = [q]
    n_keys = 0
    for k, v in kvs:
        spec = pl.BlockSpec((1,) + k.shape[1:], lambda b, i: (b, 0, 0))
        in_specs += [spec, spec]
        args += [k, v]
        n_keys += k.shape[1]
    vmem = (4 * tq * w * 2 + 4 * n_keys * kvs[0][0].shape[2] * 2
            + 2 * (2 * tq) * n_keys * (4 + 4 + 2) + (8 << 20))
    return pl.pallas_call(
        _gqa_kernel,
        grid=(b_, t // tq),
        in_specs=in_specs,
        out_specs=pl.BlockSpec((1, tq, w), lambda b, i: (b, i, 0)),
        out_shape=jax.ShapeDtypeStruct(q.shape, BF16),
        compiler_params=_params(("parallel", "parallel"), vmem),
        name="gqa",
    )(*args)


def _na_kernel(q_ref, k_ref, v_ref, kc_ref, vc_ref, a_ref, o_ref, *, rows):
    j = pl.program_id(1)
    nblk = pl.num_programs(1)
    sel = jnp.where(j == 0, 0, jnp.where(j == nblk - 1, 2, 1))
    ks = jnp.clip(j * NA_Q_ROWS - NA_WIN_H // 2, 0, rows - NA_K_ROWS)
    loc = pl.ds(pl.multiple_of(ks * GRID_W, NA_Q_ROWS * GRID_W), NA_K_ROWS * GRID_W)
    lo = _lane_lo()
    for c in range(q_ref.shape[2] // LANES):
        cols = slice(c * LANES, (c + 1) * LANES)
        q2 = _stack_heads(q_ref[0, :, cols], lo)
        blocks = [(k_ref[0, loc, cols], v_ref[0, loc, cols], a_ref[sel, c]),
                  (kc_ref[0, :, cols], vc_ref[0, :, cols], None)]
        o_ref[0, :, cols] = _unstack_heads(_attend(q2, blocks), lo).astype(BF16)


def _na_bias_tables(rpb, rows):
    nh = rpb.shape[0]
    kh = min(NA_WIN_H, rows)
    col = np.arange(GRID_W)
    cs = np.clip(col - NA_WIN_W // 2, 0, GRID_W - NA_WIN_W)
    band = (col[None, :] >= cs[:, None]) & (col[None, :] < cs[:, None] + NA_WIN_W)
    dc = np.clip(col[None, :] - col[:, None], -(NA_WIN_W - 1), NA_WIN_W - 1) + NA_WIN_W - 1
    out = []
    for r0 in (0, NA_Q_ROWS, rows - NA_Q_ROWS):
        ks = int(np.clip(r0 - NA_WIN_H // 2, 0, rows - NA_K_ROWS))
        r = r0 + np.arange(NA_Q_ROWS)
        rs = np.clip(r - kh // 2, 0, rows - kh)
        kr = ks + np.arange(NA_K_ROWS)
        row_ok = (kr[None, :] >= rs[:, None]) & (kr[None, :] < rs[:, None] + kh)
        dr = np.clip(kr[None, :] - r[:, None] + NA_WIN_H - 1, 0, 2 * NA_WIN_H - 2)
        ok = row_ok[:, None, :, None] & band[None, :, None, :]
        idx_r = np.broadcast_to(dr[:, None, :, None], ok.shape)
        idx_c = np.broadcast_to(dc[None, :, None, :], ok.shape)
        bias = rpb[:, idx_r, idx_c]
        a = jnp.where(ok[None], bias, NEG_BIG).reshape(nh, NA_Q_ROWS * GRID_W, NA_K_ROWS * GRID_W)
        out.append(a.reshape(nh // 2, 2 * NA_Q_ROWS * GRID_W, NA_K_ROWS * GRID_W))
    return jnp.stack(out).astype(F32)


def _na(q, k, v, kc, vc, a_tab):
    b_, t, w = q.shape
    tq = NA_Q_ROWS * GRID_W
    n_loc = NA_K_ROWS * GRID_W
    full = lambda a: pl.BlockSpec((1,) + a.shape[1:], lambda b, i: (b, 0, 0))
    vmem = (a_tab.size * 4 + 4 * tq * w * 2 + 4 * (k.shape[1] + kc.shape[1]) * w * 2
            + 2 * (2 * tq) * (n_loc + kc.shape[1]) * (4 + 4 + 2) + (8 << 20))
    return pl.pallas_call(
        functools.partial(_na_kernel, rows=t // GRID_W),
        grid=(b_, t // tq),
        in_specs=[pl.BlockSpec((1, tq, w), lambda b, i: (b, i, 0)), full(k), full(v), full(kc), full(vc),
                  _resident(a_tab.shape, lambda b, i: (0, 0, 0, 0))],
        out_specs=pl.BlockSpec((1, tq, w), lambda b, i: (b, i, 0)),
        out_shape=jax.ShapeDtypeStruct(q.shape, BF16),
        compiler_params=_params(("parallel", "arbitrary"), vmem),
        name="na",
    )(q, k, v, kc, vc, a_tab)


def _merge_kernel(x_ref, gt_ref, yf_ref, yg_ref, yn_ref, sg_ref, wf_ref, wg_ref, wn_ref, wo_ref, o_ref):
    d = x_ref.shape[2]
    m = (sg_ref[0, :, :d].astype(F32) * _dot(yf_ref[0], wf_ref[...])
         + sg_ref[0, :, d:2 * d].astype(F32) * _dot(yg_ref[0], wg_ref[...])
         + sg_ref[0, :, 2 * d:].astype(F32) * _dot(yn_ref[0], wn_ref[...]))
    o_ref[0] = x_ref[0] + gt_ref[0] * _dot(m.astype(BF16), wo_ref[...])


def _merge(x, mod3, row_fn, yf, yg, yn, sg, wf, wg, wn, wo, *, tm):
    b_, t, d = x.shape
    tok = lambda wd: pl.BlockSpec((1, tm, wd), lambda b, i: (b, i, 0))
    res = lambda a: _resident(a.shape, lambda b, i: (0, 0))
    vmem = (2 * (wf.size + wg.size + wn.size + wo.size) + 4 * tm * d * 4
            + 2 * tm * (yf.shape[2] + yg.shape[2] + yn.shape[2] + 3 * d) * 2 + 6 * tm * d * 4 + (8 << 20))
    return pl.pallas_call(
        _merge_kernel,
        grid=(b_, t // tm),
        in_specs=[tok(d), pl.BlockSpec((1, 1, d), lambda b, i: (row_fn(b, 5), 0, 0)),
                  tok(yf.shape[2]), tok(yg.shape[2]), tok(yn.shape[2]), tok(3 * d),
                  res(wf), res(wg), res(wn), res(wo)],
        out_specs=tok(d),
        out_shape=jax.ShapeDtypeStruct(x.shape, F32),
        compiler_params=_params(("parallel", "parallel"), vmem),
        name="merge",
    )(x, mod3, yf, yg, yn, sg, wf, wg, wn, wo)


def _rope_tables(seq_len):
    t = np.arange(seq_len)
    pairs = HEAD_DIM // 4
    freqs = jnp.asarray(ROPE_THETA, F32) ** (-jnp.arange(pairs, dtype=F32) / pairs)
    row = jnp.asarray(t // GRID_W, F32)
    col = jnp.asarray(t % GRID_W, F32)
    ang = jnp.concatenate([row[:, None] * freqs, col[:, None] * freqs], axis=-1)
    cos = jnp.repeat(jnp.cos(ang), 2, axis=-1)
    sin = jnp.repeat(jnp.sin(ang), 2, axis=-1) * jnp.asarray(np.tile([-1.0, 1.0], HEAD_DIM // 2), F32)
    reps = LANES // HEAD_DIM
    return jnp.tile(cos, (1, reps)), jnp.tile(sin, (1, reps))


def _pair_heads(w_cols, n_heads, axis):
    shape = list(w_cols.shape)
    new = shape[:axis] + [2, n_heads // 2, HEAD_DIM] + shape[axis + 1:]
    return jnp.swapaxes(w_cols.reshape(new), axis, axis + 1).reshape(shape)


def kernel(x, c, ctx, c_ctx, mod_w, mod_b, norm_ffn1, ffn1_w_in, ffn1_w_out, norm_mix, w_in, q_norm, k_norm,
           na_rpb, w_fourier, w_gqa_out, w_na_out, w_o, norm_ffn2, ffn2_w_in, ffn2_w_out, final_norm):
    bsz, s_len, d = x.shape
    c_len = ctx.shape[1]
    depth = mod_w.shape[0]
    fw, qw, kw = d // 4, d // 2, d // 8
    n_q_heads = qw // HEAD_DIM
    assert kw == LANES and n_q_heads // (kw // HEAD_DIM) == KV_GROUP
    assert s_len % (NA_Q_ROWS * GRID_W) == 0 and s_len // GRID_W >= NA_K_ROWS
    ctx_rows = 32
    assert bsz < ctx_rows

    cc = jnp.zeros((ctx_rows, d), F32).at[:bsz].set(c).at[bsz].set(c_ctx)
    mod3 = _modulation(cc, mod_w, mod_b).reshape(depth * ctx_rows * N_MOD, 1, d)

    rope = _rope_tables(s_len)
    dft_x = _dft_tables(s_len, fw)
    dft_h = _dft_tables(c_len, fw)
    ones = jnp.asarray(np.kron(np.eye(n_q_heads), np.ones((HEAD_DIM, HEAD_DIM))), BF16)

    tm = 512 if s_len % 512 == 0 else 256
    h = ctx
    hb = 4 if bsz % 4 == 0 else 1
    for l in range(depth):
        last = l == depth - 1
        row_x = lambda b, k, l=l: (l * ctx_rows + b) * N_MOD + k
        row_h = lambda b, k, l=l: (l * ctx_rows + bsz) * N_MOD + k
        bf = lambda a: a.astype(BF16)
        w1i, w1o, w2i, w2o = bf(ffn1_w_in[l]), bf(ffn1_w_out[l]), bf(ffn2_w_in[l]), bf(ffn2_w_out[l])
        wl = w_in[l]
        w_proj = bf(jnp.concatenate([wl[:, :fw], _pair_heads(wl[:, fw:fw + qw], n_q_heads, 1), wl[:, fw + qw:]], axis=1))
        w_g = bf(_pair_heads(w_gqa_out[l], n_q_heads, 0))
        w_f, w_n, w_out = bf(w_fourier[l]), bf(w_na_out[l]), bf(w_o[l])
        qg = jnp.tile(q_norm[l], n_q_heads).reshape(1, qw)
        kg = jnp.tile(k_norm[l], kw // HEAD_DIM).reshape(1, kw)
        a_tab = _na_bias_tables(na_rpb[l], s_len // GRID_W)

        x = _ffn(x, mod3, row_x, 0, norm_ffn1[l], w1i, w1o, tm=tm)
        h = _ffn(h.reshape(1, bsz * c_len, d), mod3, row_h, 0, norm_ffn1[l], w1i, w1o,
                 tm=hb * c_len).reshape(bsz, c_len, d)

        fx, gq, gk, gv, nq, nk, nv, sgx = _inproj(
            x, mod3, row_x, norm_mix[l], w_proj, qg, kg, ones, rope, nb=1, tm=tm)
        fh, gqh, gkh, gvh, nqh, nkh, nvh, sgh = _inproj(
            h, mod3, row_h, norm_mix[l], w_proj, qg, kg, ones, None, nb=hb, tm=c_len)

        yf = _fourier(fx, dft_x)
        yg = _gqa(gq, [(gk, gv), (gkh, gvh)], tq=tm)
        yn = _na(nq, nk, nv, nkh, nvh, a_tab)
        x = _merge(x, mod3, row_x, yf, yg, yn, sgx, w_f, w_g, w_n, w_out, tm=tm)
        x = _ffn(x, mod3, row_x, 6, norm_ffn2[l], w2i, w2o, tm=tm, final_gain=final_norm if last else None)
        if not last:
            yfh = _fourier(fh, dft_h)
            ygh = _gqa(gqh, [(gkh, gvh)], tq=c_len)
            ynh = _gqa(nqh, [(nkh, nvh)], tq=c_len)
            h = _merge(h, mod3, row_h, yfh, ygh, ynh, sgh, w_f, w_g, w_n, w_out, tm=c_len)
            h = _ffn(h.reshape(1, bsz * c_len, d), mod3, row_h, 6, norm_ffn2[l], w2i, w2o,
                     tm=hb * c_len).reshape(bsz, c_len, d)
    return x
```

```python
import functools
import math

import numpy as np
import jax
import jax.numpy as jnp
from jax import lax
from jax.experimental import pallas as pl
from jax.experimental.pallas import tpu as pltpu

HEAD_DIM = 64
GRID_W = 64
FOURIER_GROUPS = 4
NA_WIN_H = 8
NA_WIN_W = 16
ROPE_THETA = 10000.0
N_MOD = 9
EPS = 1e-6
KV_GROUP = 4

LANES = 128
VMEM_PHYSICAL_BYTES = 64 * 1024 * 1024

NA_Q_ROWS = 4
NA_K_ROWS = NA_Q_ROWS + NA_WIN_H
NEG_BIG = -1e30

F32 = jnp.float32
BF16 = jnp.bfloat16


def _dot(a, b):
    return jnp.dot(a, b, preferred_element_type=F32)


def _dot_nt(a, b):
    return lax.dot_general(a, b, (((1,), (1,)), ((), ())), preferred_element_type=F32)


def _params(sem, vmem_bytes):
    return pltpu.CompilerParams(dimension_semantics=sem,
                                vmem_limit_bytes=int(min(vmem_bytes, VMEM_PHYSICAL_BYTES - (4 << 20))))


def _resident(shape, index_map):
    return pl.BlockSpec(shape, index_map, pipeline_mode=pl.Buffered(1))


def _norm_mod(x, gain, shift, scale):
    y = x * lax.rsqrt(jnp.mean(x * x, axis=-1, keepdims=True) + EPS) * gain
    return y * (1.0 + scale) + shift


def _mod_kernel(c_ref, w_ref, b_ref, o_ref):
    c = c_ref[...]
    s = (c * jax.nn.sigmoid(c)).astype(BF16)
    o_ref[0] = _dot(s, w_ref[0].astype(BF16)) + b_ref[0]


def _modulation(cc, mod_w, mod_b, tn=1024):
    depth, d, n = mod_w.shape
    rows = cc.shape[0]
    return pl.pallas_call(
        _mod_kernel,
        grid=(depth, n // tn),
        in_specs=[pl.BlockSpec((rows, d), lambda l, j: (0, 0)),
                  pl.BlockSpec((1, d, tn), lambda l, j: (l, 0, j)),
                  pl.BlockSpec((1, 1, tn), lambda l, j: (l, 0, j))],
        out_specs=pl.BlockSpec((1, rows, tn), lambda l, j: (l, 0, j)),
        out_shape=jax.ShapeDtypeStruct((depth, rows, n), F32),
        compiler_params=_params(("parallel", "parallel"), 2 * (d * tn * 4) + 8 * d * tn),
        name="modulation",
    )(cc, mod_w, mod_b.reshape(depth, 1, n))


def _ffn_kernel(*refs, d_ff, tf, final):
    if final:
        x_ref, sh_ref, sc_ref, gt_ref, g_ref, win_ref, wout_ref, fg_ref, o_ref, u_scr, h_scr = refs
    else:
        x_ref, sh_ref, sc_ref, gt_ref, g_ref, win_ref, wout_ref, o_ref, u_scr, h_scr = refs
    x = x_ref[0]
    u_scr[...] = _norm_mod(x, g_ref[...], sh_ref[0], sc_ref[0]).astype(BF16)
    for j in range(d_ff // tf):
        u = u_scr[...]
        a = _dot(u, win_ref[:, j * tf:(j + 1) * tf])
        g = _dot(u, win_ref[:, d_ff + j * tf:d_ff + (j + 1) * tf])
        h_scr[:, j * tf:(j + 1) * tf] = (g * jax.nn.sigmoid(g) * a).astype(BF16)
    y = x + (0.5 * gt_ref[0]) * _dot(h_scr[...], wout_ref[...])
    if final:
        y = y * lax.rsqrt(jnp.mean(y * y, axis=-1, keepdims=True) + EPS) * fg_ref[...]
    o_ref[0] = y


def _ffn(x, mod3, row_fn, k0, gain, w_in, w_out, *, tm, final_gain=None, tf=256):
    nb, t, d = x.shape
    d_ff = w_out.shape[0]
    final = final_gain is not None
    row = lambda k: pl.BlockSpec((1, 1, d), lambda b, i: (row_fn(b, k), 0, 0))
    vec = pl.BlockSpec((1, d), lambda b, i: (0, 0))
    in_specs = [pl.BlockSpec((1, tm, d), lambda b, i: (b, i, 0)), row(k0), row(k0 + 1), row(k0 + 2), vec,
                _resident((d, 2 * d_ff), lambda b, i: (0, 0)),
                _resident((d_ff, d), lambda b, i: (0, 0))]
    args = [x, mod3, mod3, mod3, gain.reshape(1, d), w_in, w_out]
    if final:
        in_specs.append(vec)
        args.append(final_gain.reshape(1, d))
    vmem = (3 * d * d_ff * 2
            + 4 * tm * d * 4
            + tm * d * 2 + tm * d_ff * 2
            + 2 * tm * d * 4 + 4 * tm * tf * 4)
    return pl.pallas_call(
        functools.partial(_ffn_kernel, d_ff=d_ff, tf=tf, final=final),
        grid=(nb, t // tm),
        in_specs=in_specs,
        out_specs=pl.BlockSpec((1, tm, d), lambda b, i: (b, i, 0)),
        out_shape=jax.ShapeDtypeStruct(x.shape, F32),
        scratch_shapes=[pltpu.VMEM((tm, d), BF16), pltpu.VMEM((tm, d_ff), BF16)],
        compiler_params=_params(("parallel", "parallel"), vmem + (8 << 20)),
        name="ffn",
    )(*args)


def _head_norm(x, gain, ones):
    x2 = x * x
    hi = x2.astype(BF16)
    lo = (x2 - hi.astype(F32)).astype(BF16)
    ss = _dot(hi, ones) + _dot(lo, ones)
    return x * lax.rsqrt(ss * (1.0 / HEAD_DIM) + EPS) * gain


def _rope(y, cos, ssin):
    even = lax.broadcasted_iota(jnp.int32, (1, LANES), 1) % 2 == 0
    cols = []
    for c in range(y.shape[1] // LANES):
        yc = y[:, c * LANES:(c + 1) * LANES]
        partner = jnp.where(even, pltpu.roll(yc, LANES - 1, 1), pltpu.roll(yc, 1, 1))
        cols.append(yc * cos + partner * ssin)
    return cols[0] if len(cols) == 1 else jnp.concatenate(cols, axis=1)


def _inproj_kernel(*refs, rope, splits, d):
    if rope:
        (x_ref, sh_ref, sc_ref, g_ref, w_ref, qg_ref, kg_ref, ones_ref, cos_ref, sin_ref,
         f_ref, q_ref, k_ref, v_ref, nq_ref, nk_ref, nv_ref, sg_ref, u_scr) = refs
    else:
        (x_ref, sh_ref, sc_ref, g_ref, w_ref, qg_ref, kg_ref, ones_ref,
         f_ref, q_ref, k_ref, v_ref, nq_ref, nk_ref, nv_ref, sg_ref, u_scr) = refs
    m = u_scr.shape[0]
    x = x_ref[...].reshape(m, d)
    u_scr[...] = _norm_mod(x, g_ref[...], sh_ref[0], sc_ref[0]).astype(BF16)
    s_f, s_q, s_k, s_v, s_nq, s_nk, s_nv = splits
    r = _dot(u_scr[...], w_ref[:, :s_nv])
    kw = s_k - s_q
    q = _head_norm(r[:, s_f:s_q], qg_ref[...], ones_ref[...])
    k = _head_norm(r[:, s_q:s_k], kg_ref[...], ones_ref[:kw, :kw])
    if rope:
        q = _rope(q, cos_ref[...], sin_ref[...])
        k = _rope(k, cos_ref[...], sin_ref[...])
    scale = HEAD_DIM ** -0.5
    put = lambda ref, val: ref.__setitem__(Ellipsis, val.astype(BF16).reshape(ref.shape))
    put(f_ref, r[:, :s_f])
    put(q_ref, q * scale)
    put(k_ref, k)
    put(v_ref, r[:, s_k:s_v])
    put(nq_ref, r[:, s_v:s_nq] * scale)
    put(nk_ref, r[:, s_nq:s_nk])
    put(nv_ref, r[:, s_nk:s_nv])
    for c in range(3):
        g = _dot(u_scr[...], w_ref[:, s_nv + c * d:s_nv + (c + 1) * d])
        sg_ref[..., c * d:(c + 1) * d] = jax.nn.sigmoid(g).astype(BF16).reshape(sg_ref.shape[:-1] + (d,))


def _inproj(x, mod3, row_fn, gain, w, qg, kg, ones, tables, *, nb, tm):
    b_, t, d = x.shape
    n_tot = w.shape[1]
    fw, qw, kw, naw = d // 4, d // 2, d // 8, d // 4
    widths = [fw, qw, kw, kw, naw, naw, naw]
    splits = tuple(int(v) for v in np.cumsum(widths))
    rope = tables is not None
    row = lambda k: pl.BlockSpec((1, 1, d), lambda b, i: (row_fn(b, k), 0, 0))
    const = lambda shape: pl.BlockSpec(shape, lambda b, i: (0,) * len(shape))
    tok = lambda wd: pl.BlockSpec((nb, tm, wd), lambda b, i: (b, i, 0))
    in_specs = [tok(d), row(3), row(4), const((1, d)), _resident((d, n_tot), lambda b, i: (0, 0)),
                const((1, qw)), const((1, kw)), const((qw, qw))]
    args = [x, mod3, mod3, gain.reshape(1, d), w, qg, kg, ones]
    if rope:
        in_specs += [pl.BlockSpec((tm, LANES), lambda b, i: (i, 0))] * 2
        args += list(tables)
    out_widths = widths + [3 * d]
    m = nb * tm
    vmem = (d * n_tot * 2 + 2 * m * d * 4 + m * d * 2 + 2 * m * n_tot * 2
            + m * splits[-1] * 4 + 6 * m * qw * 4 + 2 * m * d * 4)
    return pl.pallas_call(
        functools.partial(_inproj_kernel, rope=rope, splits=splits, d=d),
        grid=(b_ // nb, t // tm),
        in_specs=in_specs,
        out_specs=[tok(wd) for wd in out_widths],
        out_shape=[jax.ShapeDtypeStruct((b_, t, wd), BF16) for wd in out_widths],
        scratch_shapes=[pltpu.VMEM((m, d), BF16)],
        compiler_params=_params(("parallel", "parallel"), vmem + (8 << 20)),
        name="inproj",
    )(*args)


def _fourier_kernel(x_ref, cs_ref, ct_ref, st_ref, o_ref):
    w = x_ref.shape[2]
    ab = _dot(x_ref[0], cs_ref[...]).astype(BF16)
    y = _dot(ct_ref[...], ab[:, :w]) - _dot(st_ref[...], ab[:, w:])
    o_ref[0] = y.astype(BF16)


def _dft_tables(t, w):
    def cs(n):
        i = jnp.arange(n, dtype=jnp.int32)
        ang = ((i[:, None] * i[None, :]) % n).astype(F32) * (2.0 * math.pi / n)
        return jnp.cos(ang) * (1.0 / math.sqrt(n)), jnp.sin(ang) * (1.0 / math.sqrt(n))
    ct, st = cs(t)
    cg, sg = cs(w // FOURIER_GROUPS)
    eye = jnp.eye(FOURIER_GROUPS, dtype=F32)
    cs_c = jnp.concatenate([jnp.kron(eye, cg), jnp.kron(eye, sg)], axis=1)
    return cs_c.astype(BF16), ct.astype(BF16), st.astype(BF16)


def _fourier(f, tables):
    b_, t, w = f.shape
    cs_c, ct, st = tables
    return pl.pallas_call(
        _fourier_kernel,
        grid=(b_,),
        in_specs=[pl.BlockSpec((1, t, w), lambda b: (b, 0, 0)),
                  _resident((w, 2 * w), lambda b: (0, 0)),
                  _resident((t, t), lambda b: (0, 0)),
                  _resident((t, t), lambda b: (0, 0))],
        out_specs=pl.BlockSpec((1, t, w), lambda b: (b, 0, 0)),
        out_shape=jax.ShapeDtypeStruct(f.shape, BF16),
        compiler_params=_params(("parallel",), 2 * t * t * 2 + 4 * t * w * 2 + 4 * t * w * 4 + (8 << 20)),
        name="fourier",
    )(f, cs_c, ct, st)


def _lane_lo():
    return lax.broadcasted_iota(jnp.int32, (1, LANES), 1) < HEAD_DIM


def _stack_heads(qc, lo):
    zero = jnp.zeros_like(qc)
    return jnp.concatenate([jnp.where(lo, qc, zero), jnp.where(lo, zero, qc)], axis=0)


def _unstack_heads(o, lo):
    t = o.shape[0] // 2
    return jnp.where(lo, o[:t], o[t:])


def _attend(q2, blocks):
    scores = []
    for k, _, bias in blocks:
        s = _dot_nt(q2, k)
        scores.append(s if bias is None else s + bias)
    m = functools.reduce(jnp.maximum, [jnp.max(s, axis=-1, keepdims=True) for s in scores])
    probs = [jnp.exp(s - m) for s in scores]
    l = functools.reduce(jnp.add, [jnp.sum(p, axis=-1, keepdims=True) for p in probs])
    o = functools.reduce(jnp.add, [_dot(p.astype(BF16), v) for p, (_, v, _) in zip(probs, blocks)])
    return o / l


def _gqa_kernel(*refs):
    q_ref, o_ref = refs[0], refs[-1]
    kv_refs = refs[1:-1]
    lo = _lane_lo()
    for c in range(q_ref.shape[2] // LANES):
        cols = slice(c * LANES, (c + 1) * LANES)
        q2 = _stack_heads(q_ref[0, :, cols], lo)
        blocks = []
        for k_ref, v_ref in zip(kv_refs[0::2], kv_refs[1::2]):
            kc = cols if k_ref.shape[2] > LANES else slice(0, LANES)
            blocks.append((k_ref[0, :, kc], v_ref[0, :, kc], None))
        o_ref[0, :, cols] = _unstack_heads(_attend(q2, blocks), lo).astype(BF16)


def _gqa(q, kvs, *, tq):
    b_, t, w = q.shape
    in_specs = [pl.BlockSpec((1, tq, w), lambda b, i: (b, i, 0))]
    args = [q]
    n_keys = 0
    for k, v in kvs:
        spec = pl.BlockSpec((1,) + k.shape[1:], lambda b, i: (b, 0, 0))
        in_specs += [spec, spec]
        args += [k, v]
        n_keys += k.shape[1]
    vmem = (4 * tq * w * 2 + 4 * n_keys * kvs[0][0].shape[2] * 2
            + 2 * (2 * tq) * n_keys * (4 + 4 + 2) + (8 << 20))
    return pl.pallas_call(
        _gqa_kernel,
        grid=(b_, t // tq),
        in_specs=in_specs,
        out_specs=pl.BlockSpec((1, tq, w), lambda b, i: (b, i, 0)),
        out_shape=jax.ShapeDtypeStruct(q.shape, BF16),
        compiler_params=_params(("parallel", "parallel"), vmem),
        name="gqa",
    )(*args)


def _na_kernel(q_ref, k_ref, v_ref, kc_ref, vc_ref, a_ref, o_ref, *, rows):
    j = pl.program_id(1)
    nblk = pl.num_programs(1)
    sel = jnp.where(j == 0, 0, jnp.where(j == nblk - 1, 2, 1))
    ks = jnp.clip(j * NA_Q_ROWS - NA_WIN_H // 2, 0, rows - NA_K_ROWS)
    loc = pl.ds(pl.multiple_of(ks * GRID_W, NA_Q_ROWS * GRID_W), NA_K_ROWS * GRID_W)
    lo = _lane_lo()
    for c in range(q_ref.shape[2] // LANES):
        cols = slice(c * LANES, (c + 1) * LANES)
        q2 = _stack_heads(q_ref[0, :, cols], lo)
        blocks = [(k_ref[0, loc, cols], v_ref[0, loc, cols], a_ref[sel, c]),
                  (kc_ref[0, :, cols], vc_ref[0, :, cols], None)]
        o_ref[0, :, cols] = _unstack_heads(_attend(q2, blocks), lo).astype(BF16)


def _na_bias_tables(rpb, rows):
    nh, n_dr, n_dc = rpb.shape
    tq = NA_Q_ROWS * GRID_W
    return pl.pallas_call(
        functools.partial(_na_bias_kernel, rows=rows, nh=nh, n_dr=n_dr, n_dc=n_dc),
        in_specs=[pl.BlockSpec(memory_space=pltpu.SMEM)],
        out_shape=jax.ShapeDtypeStruct((3, nh // 2, 2 * tq, NA_K_ROWS * GRID_W), F32),
        compiler_params=pltpu.CompilerParams(
            vmem_limit_bytes=2 * 3 * nh * tq * NA_K_ROWS * GRID_W * 4 + (8 << 20)),
        name="na_bias",
    )(rpb.reshape(-1))


def _na_bias_kernel(rpb_ref, o_ref, *, rows, nh, n_dr, n_dc):
    kh = min(NA_WIN_H, rows)
    qc = lax.broadcasted_iota(jnp.int32, (GRID_W, GRID_W), 0)
    kc = lax.broadcasted_iota(jnp.int32, (GRID_W, GRID_W), 1)
    cs = jnp.clip(qc - NA_WIN_W // 2, 0, GRID_W - NA_WIN_W)
    band = (kc >= cs) & (kc < cs + NA_WIN_W)
    dc = kc - qc + (NA_WIN_W - 1)
    o_ref[...] = jnp.full(o_ref.shape, NEG_BIG, F32)
    uses = {}
    for sel, r0 in enumerate((0, NA_Q_ROWS, rows - NA_Q_ROWS)):
        ks = int(np.clip(r0 - NA_WIN_H // 2, 0, rows - NA_K_ROWS))
        for qi in range(NA_Q_ROWS):
            r = r0 + qi
            rs = int(np.clip(r - kh // 2, 0, rows - kh))
            for ki in range(NA_K_ROWS):
                kr = ks + ki
                if rs <= kr < rs + kh:
                    uses.setdefault(kr - r + NA_WIN_H - 1, []).append((sel, qi, ki))
    for h in range(nh):
        c, e = divmod(h, 2)
        for dr, dests in sorted(uses.items()):
            tile = jnp.full((GRID_W, GRID_W), NEG_BIG, F32)
            for j in range(n_dc):
                tile = jnp.where(band & (dc == j), rpb_ref[(h * n_dr + dr) * n_dc + j], tile)
            for sel, qi, ki in dests:
                r_lo = (e * NA_Q_ROWS + qi) * GRID_W
                o_ref[sel, c, r_lo:r_lo + GRID_W, ki * GRID_W:(ki + 1) * GRID_W] = tile


def _na(q, k, v, kc, vc, a_tab):
    b_, t, w = q.shape
    tq = NA_Q_ROWS * GRID_W
    n_loc = NA_K_ROWS * GRID_W
    full = lambda a: pl.BlockSpec((1,) + a.shape[1:], lambda b, i: (b, 0, 0))
    vmem = (a_tab.size * 4 + 4 * tq * w * 2 + 4 * (k.shape[1] + kc.shape[1]) * w * 2
            + 2 * (2 * tq) * (n_loc + kc.shape[1]) * (4 + 4 + 2) + (8 << 20))
    return pl.pallas_call(
        functools.partial(_na_kernel, rows=t // GRID_W),
        grid=(b_, t // tq),
        in_specs=[pl.BlockSpec((1, tq, w), lambda b, i: (b, i, 0)), full(k), full(v), full(kc), full(vc),
                  _resident(a_tab.shape, lambda b, i: (0, 0, 0, 0))],
        out_specs=pl.BlockSpec((1, tq, w), lambda b, i: (b, i, 0)),
        out_shape=jax.ShapeDtypeStruct(q.shape, BF16),
        compiler_params=_params(("parallel", "arbitrary"), vmem),
        name="na",
    )(q, k, v, kc, vc, a_tab)


def _merge_kernel(x_ref, gt_ref, yf_ref, yg_ref, yn_ref, sg_ref, wf_ref, wg_ref, wn_ref, wo_ref, o_ref):
    d = x_ref.shape[2]
    m = (sg_ref[0, :, :d].astype(F32) * _dot(yf_ref[0], wf_ref[...])
         + sg_ref[0, :, d:2 * d].astype(F32) * _dot(yg_ref[0], wg_ref[...])
         + sg_ref[0, :, 2 * d:].astype(F32) * _dot(yn_ref[0], wn_ref[...]))
    o_ref[0] = x_ref[0] + gt_ref[0] * _dot(m.astype(BF16), wo_ref[...])


def _merge(x, mod3, row_fn, yf, yg, yn, sg, wf, wg, wn, wo, *, tm):
    b_, t, d = x.shape
    tok = lambda wd: pl.BlockSpec((1, tm, wd), lambda b, i: (b, i, 0))
    res = lambda a: _resident(a.shape, lambda b, i: (0, 0))
    vmem = (2 * (wf.size + wg.size + wn.size + wo.size) + 4 * tm * d * 4
            + 2 * tm * (yf.shape[2] + yg.shape[2] + yn.shape[2] + 3 * d) * 2 + 6 * tm * d * 4 + (8 << 20))
    return pl.pallas_call(
        _merge_kernel,
        grid=(b_, t // tm),
        in_specs=[tok(d), pl.BlockSpec((1, 1, d), lambda b, i: (row_fn(b, 5), 0, 0)),
                  tok(yf.shape[2]), tok(yg.shape[2]), tok(yn.shape[2]), tok(3 * d),
                  res(wf), res(wg), res(wn), res(wo)],
        out_specs=tok(d),
        out_shape=jax.ShapeDtypeStruct(x.shape, F32),
        compiler_params=_params(("parallel", "parallel"), vmem),
        name="merge",
    )(x, mod3, yf, yg, yn, sg, wf, wg, wn, wo)


def _rope_tables(seq_len):
    t = np.arange(seq_len)
    pairs = HEAD_DIM // 4
    freqs = jnp.asarray(ROPE_THETA, F32) ** (-jnp.arange(pairs, dtype=F32) / pairs)
    row = jnp.asarray(t // GRID_W, F32)
    col = jnp.asarray(t % GRID_W, F32)
    ang = jnp.concatenate([row[:, None] * freqs, col[:, None] * freqs], axis=-1)
    cos = jnp.repeat(jnp.cos(ang), 2, axis=-1)
    sin = jnp.repeat(jnp.sin(ang), 2, axis=-1) * jnp.asarray(np.tile([-1.0, 1.0], HEAD_DIM // 2), F32)
    reps = LANES // HEAD_DIM
    return jnp.tile(cos, (1, reps)), jnp.tile(sin, (1, reps))


def _pair_heads(w_cols, n_heads, axis):
    shape = list(w_cols.shape)
    new = shape[:axis] + [2, n_heads // 2, HEAD_DIM] + shape[axis + 1:]
    return jnp.swapaxes(w_cols.reshape(new), axis, axis + 1).reshape(shape)


def kernel(x, c, ctx, c_ctx, mod_w, mod_b, norm_ffn1, ffn1_w_in, ffn1_w_out, norm_mix, w_in, q_norm, k_norm,
           na_rpb, w_fourier, w_gqa_out, w_na_out, w_o, norm_ffn2, ffn2_w_in, ffn2_w_out, final_norm):
    bsz, s_len, d = x.shape
    c_len = ctx.shape[1]
    depth = mod_w.shape[0]
    fw, qw, kw = d // 4, d // 2, d // 8
    n_q_heads = qw // HEAD_DIM
    assert kw == LANES and n_q_heads // (kw // HEAD_DIM) == KV_GROUP
    assert s_len % (NA_Q_ROWS * GRID_W) == 0 and s_len // GRID_W >= NA_K_ROWS
    ctx_rows = 32
    assert bsz < ctx_rows

    cc = jnp.zeros((ctx_rows, d), F32).at[:bsz].set(c).at[bsz].set(c_ctx)
    mod3 = _modulation(cc, mod_w, mod_b).reshape(depth * ctx_rows * N_MOD, 1, d)

    rope = _rope_tables(s_len)
    dft_x = _dft_tables(s_len, fw)
    dft_h = _dft_tables(c_len, fw)
    ones = jnp.asarray(np.kron(np.eye(n_q_heads), np.ones((HEAD_DIM, HEAD_DIM))), BF16)

    tm = 512 if s_len % 512 == 0 else 256
    h = ctx
    hb = 4 if bsz % 4 == 0 else 1
    for l in range(depth):
        last = l == depth - 1
        row_x = lambda b, k, l=l: (l * ctx_rows + b) * N_MOD + k
        row_h = lambda b, k, l=l: (l * ctx_rows + bsz) * N_MOD + k
        bf = lambda a: a.astype(BF16)
        w1i, w1o, w2i, w2o = bf(ffn1_w_in[l]), bf(ffn1_w_out[l]), bf(ffn2_w_in[l]), bf(ffn2_w_out[l])
        wl = w_in[l]
        w_proj = bf(jnp.concatenate([wl[:, :fw], _pair_heads(wl[:, fw:fw + qw], n_q_heads, 1), wl[:, fw + qw:]], axis=1))
        w_g = bf(_pair_heads(w_gqa_out[l], n_q_heads, 0))
        w_f, w_n, w_out = bf(w_fourier[l]), bf(w_na_out[l]), bf(w_o[l])
        qg = jnp.tile(q_norm[l], n_q_heads).reshape(1, qw)
        kg = jnp.tile(k_norm[l], kw // HEAD_DIM).reshape(1, kw)
        a_tab = _na_bias_tables(na_rpb[l], s_len // GRID_W)

        x = _ffn(x, mod3, row_x, 0, norm_ffn1[l], w1i, w1o, tm=tm)
        h = _ffn(h.reshape(1, bsz * c_len, d), mod3, row_h, 0, norm_ffn1[l], w1i, w1o,
                 tm=hb * c_len).reshape(bsz, c_len, d)

        fx, gq, gk, gv, nq, nk, nv, sgx = _inproj(
            x, mod3, row_x, norm_mix[l], w_proj, qg, kg, ones, rope, nb=1, tm=tm)
        fh, gqh, gkh, gvh, nqh, nkh, nvh, sgh = _inproj(
            h, mod3, row_h, norm_mix[l], w_proj, qg, kg, ones, None, nb=hb, tm=c_len)

        yf = _fourier(fx, dft_x)
        yg = _gqa(gq, [(gk, gv), (gkh, gvh)], tq=tm)
        yn = _na(nq, nk, nv, nkh, nvh, a_tab)
        x = _merge(x, mod3, row_x, yf, yg, yn, sgx, w_f, w_g, w_n, w_out, tm=tm)
        x = _ffn(x, mod3, row_x, 6, norm_ffn2[l], w2i, w2o, tm=tm, final_gain=final_norm if last else None)
        if not last:
            yfh = _fourier(fh, dft_h)
            ygh = _gqa(gqh, [(gkh, gvh)], tq=c_len)
            ynh = _gqa(nqh, [(nkh, nvh)], tq=c_len)
            h = _merge(h, mod3, row_h, yfh, ygh, ynh, sgh, w_f, w_g, w_n, w_out, tm=c_len)
            h = _ffn(h.reshape(1, bsz * c_len, d), mod3, row_h, 6, norm_ffn2[l], w2i, w2o,
                     tm=hb * c_len).reshape(bsz, c_len, d)
    return x
```

```python
import functools
import math

import numpy as np
import jax
import jax.numpy as jnp
from jax import lax
from jax.experimental import pallas as pl
from jax.experimental.pallas import tpu as pltpu

HEAD_DIM = 64
GRID_W = 64
FOURIER_GROUPS = 4
NA_WIN_H = 8
NA_WIN_W = 16
ROPE_THETA = 10000.0
N_MOD = 9
EPS = 1e-6
KV_GROUP = 4

LANES = 128
VMEM_PHYSICAL_BYTES = 64 * 1024 * 1024

NA_Q_ROWS = 4
NA_K_ROWS = NA_Q_ROWS + NA_WIN_H
NEG_BIG = -1e30
LOG2_E = math.log2(math.e)
SCORE_SCALE = HEAD_DIM ** -0.5 * LOG2_E

F32 = jnp.float32
BF16 = jnp.bfloat16


def _dot(a, b):
    return jnp.dot(a, b, preferred_element_type=F32)


def _dot_nt(a, b):
    return lax.dot_general(a, b, (((1,), (1,)), ((), ())), preferred_element_type=F32)


def _params(sem, vmem_bytes):
    return pltpu.CompilerParams(dimension_semantics=sem,
                                vmem_limit_bytes=int(min(vmem_bytes, VMEM_PHYSICAL_BYTES - (4 << 20))))


def _resident(shape, index_map):
    return pl.BlockSpec(shape, index_map, pipeline_mode=pl.Buffered(1))


def _norm_mod(x, gain, shift, scale):
    y = x * lax.rsqrt(jnp.mean(x * x, axis=-1, keepdims=True) + EPS) * gain
    return y * (1.0 + scale) + shift


def _mod_kernel(c_ref, w_ref, b_ref, o_ref):
    c = c_ref[...]
    s = (c * jax.nn.sigmoid(c)).astype(BF16)
    o_ref[0] = _dot(s, w_ref[0].astype(BF16)) + b_ref[0]


def _modulation(cc, mod_w, mod_b, tn=1024):
    depth, d, n = mod_w.shape
    rows = cc.shape[0]
    return pl.pallas_call(
        _mod_kernel,
        grid=(depth, n // tn),
        in_specs=[pl.BlockSpec((rows, d), lambda l, j: (0, 0)),
                  pl.BlockSpec((1, d, tn), lambda l, j: (l, 0, j)),
                  pl.BlockSpec((1, 1, tn), lambda l, j: (l, 0, j))],
        out_specs=pl.BlockSpec((1, rows, tn), lambda l, j: (l, 0, j)),
        out_shape=jax.ShapeDtypeStruct((depth, rows, n), F32),
        compiler_params=_params(("parallel", "parallel"), 2 * (d * tn * 4) + 8 * d * tn),
        name="modulation",
    )(cc, mod_w, mod_b.reshape(depth, 1, n))


def _ffn_kernel(*refs, d_ff, tf, final):
    if final:
        x_ref, sh_ref, sc_ref, gt_ref, g_ref, win_ref, wout_ref, fg_ref, o_ref, u_scr, h_scr = refs
    else:
        x_ref, sh_ref, sc_ref, gt_ref, g_ref, win_ref, wout_ref, o_ref, u_scr, h_scr = refs
    x = x_ref[0]
    u_scr[...] = _norm_mod(x, g_ref[...], sh_ref[0], sc_ref[0]).astype(BF16)
    for j in range(d_ff // tf):
        u = u_scr[...]
        a = _dot(u, win_ref[:, j * tf:(j + 1) * tf])
        g = _dot(u, win_ref[:, d_ff + j * tf:d_ff + (j + 1) * tf])
        h_scr[:, j * tf:(j + 1) * tf] = (g * jax.nn.sigmoid(g) * a).astype(BF16)
    y = x + (0.5 * gt_ref[0]) * _dot(h_scr[...], wout_ref[...])
    if final:
        y = y * lax.rsqrt(jnp.mean(y * y, axis=-1, keepdims=True) + EPS) * fg_ref[...]
    o_ref[0] = y


def _ffn(x, mod3, row_fn, k0, gain, w_in, w_out, *, tm, final_gain=None, tf=256):
    nb, t, d = x.shape
    d_ff = w_out.shape[0]
    final = final_gain is not None
    row = lambda k: pl.BlockSpec((1, 1, d), lambda b, i: (row_fn(b, k), 0, 0))
    vec = pl.BlockSpec((1, d), lambda b, i: (0, 0))
    in_specs = [pl.BlockSpec((1, tm, d), lambda b, i: (b, i, 0)), row(k0), row(k0 + 1), row(k0 + 2), vec,
                _resident((d, 2 * d_ff), lambda b, i: (0, 0)),
                _resident((d_ff, d), lambda b, i: (0, 0))]
    args = [x, mod3, mod3, mod3, gain.reshape(1, d), w_in, w_out]
    if final:
        in_specs.append(vec)
        args.append(final_gain.reshape(1, d))
    vmem = (3 * d * d_ff * 2
            + 4 * tm * d * 4
            + tm * d * 2 + tm * d_ff * 2
            + 2 * tm * d * 4 + 4 * tm * tf * 4)
    return pl.pallas_call(
        functools.partial(_ffn_kernel, d_ff=d_ff, tf=tf, final=final),
        grid=(nb, t // tm),
        in_specs=in_specs,
        out_specs=pl.BlockSpec((1, tm, d), lambda b, i: (b, i, 0)),
        out_shape=jax.ShapeDtypeStruct(x.shape, F32),
        scratch_shapes=[pltpu.VMEM((tm, d), BF16), pltpu.VMEM((tm, d_ff), BF16)],
        compiler_params=_params(("parallel", "parallel"), vmem + (8 << 20)),
        name="ffn",
    )(*args)


def _head_norm(x, gain, ones):
    x2 = x * x
    hi = x2.astype(BF16)
    lo = (x2 - hi.astype(F32)).astype(BF16)
    ss = _dot(hi, ones) + _dot(lo, ones)
    return x * lax.rsqrt(ss * (1.0 / HEAD_DIM) + EPS) * gain


def _rope(y, cos, ssin):
    even = lax.broadcasted_iota(jnp.int32, (1, LANES), 1) % 2 == 0
    cols = []
    for c in range(y.shape[1] // LANES):
        yc = y[:, c * LANES:(c + 1) * LANES]
        partner = jnp.where(even, pltpu.roll(yc, LANES - 1, 1), pltpu.roll(yc, 1, 1))
        cols.append(yc * cos + partner * ssin)
    return cols[0] if len(cols) == 1 else jnp.concatenate(cols, axis=1)


def _inproj_kernel(*refs, rope, splits, d):
    if rope:
        (x_ref, sh_ref, sc_ref, g_ref, w_ref, qg_ref, kg_ref, ones_ref, cos_ref, sin_ref,
         f_ref, q_ref, k_ref, v_ref, nq_ref, nk_ref, nv_ref, sg_ref, u_scr) = refs
    else:
        (x_ref, sh_ref, sc_ref, g_ref, w_ref, qg_ref, kg_ref, ones_ref,
         f_ref, q_ref, k_ref, v_ref, nq_ref, nk_ref, nv_ref, sg_ref, u_scr) = refs
    m = u_scr.shape[0]
    x = x_ref[...].reshape(m, d)
    u_scr[...] = _norm_mod(x, g_ref[...], sh_ref[0], sc_ref[0]).astype(BF16)
    s_f, s_q, s_k, s_v, s_nq, s_nk, s_nv = splits
    r = _dot(u_scr[...], w_ref[:, :s_nv])
    kw = s_k - s_q
    q = _head_norm(r[:, s_f:s_q], qg_ref[...], ones_ref[...])
    k = _head_norm(r[:, s_q:s_k], kg_ref[...], ones_ref[:kw, :kw])
    if rope:
        q = _rope(q, cos_ref[...], sin_ref[...])
        k = _rope(k, cos_ref[...], sin_ref[...])
    scale = SCORE_SCALE
    put = lambda ref, val: ref.__setitem__(Ellipsis, val.astype(BF16).reshape(ref.shape))
    put(f_ref, r[:, :s_f])
    put(q_ref, q * scale)
    put(k_ref, k)
    put(v_ref, r[:, s_k:s_v])
    put(nq_ref, r[:, s_v:s_nq] * scale)
    put(nk_ref, r[:, s_nq:s_nk])
    put(nv_ref, r[:, s_nk:s_nv])
    for c in range(3):
        g = _dot(u_scr[...], w_ref[:, s_nv + c * d:s_nv + (c + 1) * d])
        sg_ref[..., c * d:(c + 1) * d] = jax.nn.sigmoid(g).astype(BF16).reshape(sg_ref.shape[:-1] + (d,))


def _inproj(x, mod3, row_fn, gain, w, qg, kg, ones, tables, *, nb, tm):
    b_, t, d = x.shape
    n_tot = w.shape[1]
    fw, qw, kw, naw = d // 4, d // 2, d // 8, d // 4
    widths = [fw, qw, kw, kw, naw, naw, naw]
    splits = tuple(int(v) for v in np.cumsum(widths))
    rope = tables is not None
    row = lambda k: pl.BlockSpec((1, 1, d), lambda b, i: (row_fn(b, k), 0, 0))
    const = lambda shape: pl.BlockSpec(shape, lambda b, i: (0,) * len(shape))
    tok = lambda wd: pl.BlockSpec((nb, tm, wd), lambda b, i: (b, i, 0))
    in_specs = [tok(d), row(3), row(4), const((1, d)), _resident((d, n_tot), lambda b, i: (0, 0)),
                const((1, qw)), const((1, kw)), const((qw, qw))]
    args = [x, mod3, mod3, gain.reshape(1, d), w, qg, kg, ones]
    if rope:
        in_specs += [pl.BlockSpec((tm, LANES), lambda b, i: (i, 0))] * 2
        args += list(tables)
    out_widths = widths + [3 * d]
    m = nb * tm
    vmem = (d * n_tot * 2 + 2 * m * d * 4 + m * d * 2 + 2 * m * n_tot * 2
            + m * splits[-1] * 4 + 6 * m * qw * 4 + 2 * m * d * 4)
    return pl.pallas_call(
        functools.partial(_inproj_kernel, rope=rope, splits=splits, d=d),
        grid=(b_ // nb, t // tm),
        in_specs=in_specs,
        out_specs=[tok(wd) for wd in out_widths],
        out_shape=[jax.ShapeDtypeStruct((b_, t, wd), BF16) for wd in out_widths],
        scratch_shapes=[pltpu.VMEM((m, d), BF16)],
        compiler_params=_params(("parallel", "parallel"), vmem + (8 << 20)),
        name="inproj",
    )(*args)


def _fourier_kernel(x_ref, cs_ref, ct_ref, st_ref, o_ref):
    w = x_ref.shape[2]
    ab = _dot(x_ref[0], cs_ref[...]).astype(BF16)
    y = _dot(ct_ref[...], ab[:, :w]) - _dot(st_ref[...], ab[:, w:])
    o_ref[0] = y.astype(BF16)


def _dft_tables(t, w):
    def cs(n):
        i = jnp.arange(n, dtype=jnp.int32)
        ang = ((i[:, None] * i[None, :]) % n).astype(F32) * (2.0 * math.pi / n)
        return jnp.cos(ang) * (1.0 / math.sqrt(n)), jnp.sin(ang) * (1.0 / math.sqrt(n))
    ct, st = cs(t)
    cg, sg = cs(w // FOURIER_GROUPS)
    eye = jnp.eye(FOURIER_GROUPS, dtype=F32)
    cs_c = jnp.concatenate([jnp.kron(eye, cg), jnp.kron(eye, sg)], axis=1)
    return cs_c.astype(BF16), ct.astype(BF16), st.astype(BF16)


def _fourier(f, tables):
    b_, t, w = f.shape
    cs_c, ct, st = tables
    return pl.pallas_call(
        _fourier_kernel,
        grid=(b_,),
        in_specs=[pl.BlockSpec((1, t, w), lambda b: (b, 0, 0)),
                  _resident((w, 2 * w), lambda b: (0, 0)),
                  _resident((t, t), lambda b: (0, 0)),
                  _resident((t, t), lambda b: (0, 0))],
        out_specs=pl.BlockSpec((1, t, w), lambda b: (b, 0, 0)),
        out_shape=jax.ShapeDtypeStruct(f.shape, BF16),
        compiler_params=_params(("parallel",), 2 * t * t * 2 + 4 * t * w * 2 + 4 * t * w * 4 + (8 << 20)),
        name="fourier",
    )(f, cs_c, ct, st)


def _lane_lo():
    return lax.broadcasted_iota(jnp.int32, (1, LANES), 1) < HEAD_DIM


def _stack_heads(qc, lo):
    zero = jnp.zeros_like(qc)
    return jnp.concatenate([jnp.where(lo, qc, zero), jnp.where(lo, zero, qc)], axis=0)


def _unstack_heads(o, lo):
    t = o.shape[0] // 2
    return jnp.where(lo, o[:t], o[t:])


def _attend(q2, blocks):
    scores = []
    for k, _, bias in blocks:
        s = _dot_nt(q2, k)
        scores.append(s if bias is None else s + bias)
    m = functools.reduce(jnp.maximum, [jnp.max(s, axis=-1, keepdims=True) for s in scores])
    o = functools.reduce(jnp.add, [_dot(jnp.exp2(s - m).astype(BF16), v) for s, (_, v, _) in zip(scores, blocks)])
    return o[:, :LANES] / o[:, LANES:]


def _with_ones(v):
    return jnp.concatenate([v, jnp.ones_like(v)], axis=1)


def _gqa_kernel(*refs, sub):
    q_ref, o_ref = refs[0], refs[-1]
    kv_refs = refs[1:-1]
    lo = _lane_lo()
    pairs = list(zip(kv_refs[0::2], kv_refs[1::2]))
    shared = pairs[0][0].shape[2] == LANES
    if shared:
        kv = [(k_ref[0], _with_ones(v_ref[0]), None) for k_ref, v_ref in pairs]
    for r in range(q_ref.shape[1] // sub):
        rows = slice(r * sub, (r + 1) * sub)
        for c in range(q_ref.shape[2] // LANES):
            cols = slice(c * LANES, (c + 1) * LANES)
            q2 = _stack_heads(q_ref[0, rows, cols], lo)
            blocks = kv if shared else [(k_ref[0, :, cols], _with_ones(v_ref[0, :, cols]), None)
                                        for k_ref, v_ref in pairs]
            o_ref[0, rows, cols] = _unstack_heads(_attend(q2, blocks), lo).astype(BF16)


def _gqa(q, kvs, *, tq, sub):
    b_, t, w = q.shape
    in_specs = [pl.BlockSpec((1, tq, w), lambda b, i: (b, i, 0))]
    args = [q]
    n_keys = 0
    for k, v in kvs:
        spec = pl.BlockSpec((1,) + k.shape[1:], lambda b, i: (b, 0, 0))
        in_specs += [spec, spec]
        args += [k, v]
        n_keys += k.shape[1]
    vmem = (4 * tq * w * 2 + 4 * n_keys * kvs[0][0].shape[2] * 2
            + 3 * (2 * sub) * n_keys * (4 + 4 + 2) + (8 << 20))
    return pl.pallas_call(
        functools.partial(_gqa_kernel, sub=sub),
        grid=(b_, t // tq),
        in_specs=in_specs,
        out_specs=pl.BlockSpec((1, tq, w), lambda b, i: (b, i, 0)),
        out_shape=jax.ShapeDtypeStruct(q.shape, BF16),
        compiler_params=_params(("parallel", "parallel"), vmem),
        name="gqa",
    )(*args)


def _na_kernel(q_ref, k_ref, v_ref, kc_ref, vc_ref, a_ref, o_ref, *, rows):
    tq = NA_Q_ROWS * GRID_W
    per_step = q_ref.shape[1] // tq
    nblk = rows // NA_Q_ROWS
    lo = _lane_lo()
    for r in range(per_step):
        j = pl.program_id(1) * per_step + r
        sel = jnp.where(j == 0, 0, jnp.where(j == nblk - 1, 2, 1))
        ks = jnp.clip(j * NA_Q_ROWS - NA_WIN_H // 2, 0, rows - NA_K_ROWS)
        loc = pl.ds(pl.multiple_of(ks * GRID_W, NA_Q_ROWS * GRID_W), NA_K_ROWS * GRID_W)
        q_rows = slice(r * tq, (r + 1) * tq)
        for c in range(q_ref.shape[2] // LANES):
            cols = slice(c * LANES, (c + 1) * LANES)
            q2 = _stack_heads(q_ref[0, q_rows, cols], lo)
            blocks = [(k_ref[0, loc, cols], _with_ones(v_ref[0, loc, cols]), a_ref[sel, c]),
                      (kc_ref[0, :, cols], _with_ones(vc_ref[0, :, cols]), None)]
            o_ref[0, q_rows, cols] = _unstack_heads(_attend(q2, blocks), lo).astype(BF16)


def _na_bias_tables(rpb, rows):
    nh, n_dr, n_dc = rpb.shape
    tq = NA_Q_ROWS * GRID_W
    return pl.pallas_call(
        functools.partial(_na_bias_kernel, rows=rows, nh=nh, n_dr=n_dr, n_dc=n_dc),
        in_specs=[pl.BlockSpec(memory_space=pltpu.SMEM)],
        out_shape=jax.ShapeDtypeStruct((3, nh // 2, 2 * tq, NA_K_ROWS * GRID_W), F32),
        compiler_params=pltpu.CompilerParams(
            vmem_limit_bytes=2 * 3 * nh * tq * NA_K_ROWS * GRID_W * 4 + (8 << 20)),
        name="na_bias",
    )(rpb.reshape(-1))


def _na_bias_kernel(rpb_ref, o_ref, *, rows, nh, n_dr, n_dc):
    kh = min(NA_WIN_H, rows)
    qc = lax.broadcasted_iota(jnp.int32, (GRID_W, GRID_W), 0)
    kc = lax.broadcasted_iota(jnp.int32, (GRID_W, GRID_W), 1)
    cs = jnp.clip(qc - NA_WIN_W // 2, 0, GRID_W - NA_WIN_W)
    band = (kc >= cs) & (kc < cs + NA_WIN_W)
    dc = kc - qc + (NA_WIN_W - 1)
    o_ref[...] = jnp.full(o_ref.shape, NEG_BIG, F32)
    uses = {}
    for sel, r0 in enumerate((0, NA_Q_ROWS, rows - NA_Q_ROWS)):
        ks = int(np.clip(r0 - NA_WIN_H // 2, 0, rows - NA_K_ROWS))
        for qi in range(NA_Q_ROWS):
            r = r0 + qi
            rs = int(np.clip(r - kh // 2, 0, rows - kh))
            for ki in range(NA_K_ROWS):
                kr = ks + ki
                if rs <= kr < rs + kh:
                    uses.setdefault(kr - r + NA_WIN_H - 1, []).append((sel, qi, ki))
    for h in range(nh):
        c, e = divmod(h, 2)
        for dr, dests in sorted(uses.items()):
            tile = jnp.full((GRID_W, GRID_W), NEG_BIG, F32)
            for j in range(n_dc):
                tile = jnp.where(band & (dc == j), rpb_ref[(h * n_dr + dr) * n_dc + j] * LOG2_E, tile)
            for sel, qi, ki in dests:
                r_lo = (e * NA_Q_ROWS + qi) * GRID_W
                o_ref[sel, c, r_lo:r_lo + GRID_W, ki * GRID_W:(ki + 1) * GRID_W] = tile


def _na(q, k, v, kc, vc, a_tab, *, blocks_per_step):
    b_, t, w = q.shape
    blk = NA_Q_ROWS * GRID_W
    tq = blocks_per_step * blk
    n_loc = NA_K_ROWS * GRID_W
    full = lambda a: pl.BlockSpec((1,) + a.shape[1:], lambda b, i: (b, 0, 0))
    vmem = (a_tab.size * 4 + 4 * tq * w * 2 + 4 * (k.shape[1] + kc.shape[1]) * w * 2
            + 3 * (2 * blk) * (n_loc + kc.shape[1]) * (4 + 4 + 2) + (8 << 20))
    return pl.pallas_call(
        functools.partial(_na_kernel, rows=t // GRID_W),
        grid=(b_, t // tq),
        in_specs=[pl.BlockSpec((1, tq, w), lambda b, i: (b, i, 0)), full(k), full(v), full(kc), full(vc),
                  _resident(a_tab.shape, lambda b, i: (0, 0, 0, 0))],
        out_specs=pl.BlockSpec((1, tq, w), lambda b, i: (b, i, 0)),
        out_shape=jax.ShapeDtypeStruct(q.shape, BF16),
        compiler_params=_params(("parallel", "arbitrary"), vmem),
        name="na",
    )(q, k, v, kc, vc, a_tab)


def _merge_kernel(x_ref, gt_ref, yf_ref, yg_ref, yn_ref, sg_ref, wf_ref, wg_ref, wn_ref, wo_ref, o_ref):
    d = x_ref.shape[2]
    m = (sg_ref[0, :, :d].astype(F32) * _dot(yf_ref[0], wf_ref[...])
         + sg_ref[0, :, d:2 * d].astype(F32) * _dot(yg_ref[0], wg_ref[...])
         + sg_ref[0, :, 2 * d:].astype(F32) * _dot(yn_ref[0], wn_ref[...]))
    o_ref[0] = x_ref[0] + gt_ref[0] * _dot(m.astype(BF16), wo_ref[...])


def _merge(x, mod3, row_fn, yf, yg, yn, sg, wf, wg, wn, wo, *, tm):
    b_, t, d = x.shape
    tok = lambda wd: pl.BlockSpec((1, tm, wd), lambda b, i: (b, i, 0))
    res = lambda a: _resident(a.shape, lambda b, i: (0, 0))
    vmem = (2 * (wf.size + wg.size + wn.size + wo.size) + 4 * tm * d * 4
            + 2 * tm * (yf.shape[2] + yg.shape[2] + yn.shape[2] + 3 * d) * 2 + 6 * tm * d * 4 + (8 << 20))
    return pl.pallas_call(
        _merge_kernel,
        grid=(b_, t // tm),
        in_specs=[tok(d), pl.BlockSpec((1, 1, d), lambda b, i: (row_fn(b, 5), 0, 0)),
                  tok(yf.shape[2]), tok(yg.shape[2]), tok(yn.shape[2]), tok(3 * d),
                  res(wf), res(wg), res(wn), res(wo)],
        out_specs=tok(d),
        out_shape=jax.ShapeDtypeStruct(x.shape, F32),
        compiler_params=_params(("parallel", "parallel"), vmem),
        name="merge",
    )(x, mod3, yf, yg, yn, sg, wf, wg, wn, wo)


def _rope_tables(seq_len):
    t = np.arange(seq_len)
    pairs = HEAD_DIM // 4
    freqs = jnp.asarray(ROPE_THETA, F32) ** (-jnp.arange(pairs, dtype=F32) / pairs)
    row = jnp.asarray(t // GRID_W, F32)
    col = jnp.asarray(t % GRID_W, F32)
    ang = jnp.concatenate([row[:, None] * freqs, col[:, None] * freqs], axis=-1)
    cos = jnp.repeat(jnp.cos(ang), 2, axis=-1)
    sin = jnp.repeat(jnp.sin(ang), 2, axis=-1) * jnp.asarray(np.tile([-1.0, 1.0], HEAD_DIM // 2), F32)
    reps = LANES // HEAD_DIM
    return jnp.tile(cos, (1, reps)), jnp.tile(sin, (1, reps))


def _pair_heads(w_cols, n_heads, axis):
    shape = list(w_cols.shape)
    new = shape[:axis] + [2, n_heads // 2, HEAD_DIM] + shape[axis + 1:]
    return jnp.swapaxes(w_cols.reshape(new), axis, axis + 1).reshape(shape)


def kernel(x, c, ctx, c_ctx, mod_w, mod_b, norm_ffn1, ffn1_w_in, ffn1_w_out, norm_mix, w_in, q_norm, k_norm,
           na_rpb, w_fourier, w_gqa_out, w_na_out, w_o, norm_ffn2, ffn2_w_in, ffn2_w_out, final_norm):
    bsz, s_len, d = x.shape
    c_len = ctx.shape[1]
    depth = mod_w.shape[0]
    fw, qw, kw = d // 4, d // 2, d // 8
    n_q_heads = qw // HEAD_DIM
    assert kw == LANES and n_q_heads // (kw // HEAD_DIM) == KV_GROUP
    assert s_len % (NA_Q_ROWS * GRID_W) == 0 and s_len // GRID_W >= NA_K_ROWS
    ctx_rows = 32
    assert bsz < ctx_rows

    cc = jnp.zeros((ctx_rows, d), F32).at[:bsz].set(c).at[bsz].set(c_ctx)
    mod3 = _modulation(cc, mod_w, mod_b).reshape(depth * ctx_rows * N_MOD, 1, d)

    rope = _rope_tables(s_len)
    dft_x = _dft_tables(s_len, fw)
    dft_h = _dft_tables(c_len, fw)
    ones = jnp.asarray(np.kron(np.eye(n_q_heads), np.ones((HEAD_DIM, HEAD_DIM))), BF16)

    tm = 512 if s_len % 512 == 0 else 256
    tq_att = 1024 if s_len % 1024 == 0 else 256
    na_bps = 2 if s_len % (2 * NA_Q_ROWS * GRID_W) == 0 else 1
    h = ctx
    hb = 4 if bsz % 4 == 0 else 1
    for l in range(depth):
        last = l == depth - 1
        row_x = lambda b, k, l=l: (l * ctx_rows + b) * N_MOD + k
        row_h = lambda b, k, l=l: (l * ctx_rows + bsz) * N_MOD + k
        bf = lambda a: a.astype(BF16)
        w1i, w1o, w2i, w2o = bf(ffn1_w_in[l]), bf(ffn1_w_out[l]), bf(ffn2_w_in[l]), bf(ffn2_w_out[l])
        wl = w_in[l]
        w_proj = bf(jnp.concatenate([wl[:, :fw], _pair_heads(wl[:, fw:fw + qw], n_q_heads, 1), wl[:, fw + qw:]], axis=1))
        w_g = bf(_pair_heads(w_gqa_out[l], n_q_heads, 0))
        w_f, w_n, w_out = bf(w_fourier[l]), bf(w_na_out[l]), bf(w_o[l])
        qg = jnp.tile(q_norm[l], n_q_heads).reshape(1, qw)
        kg = jnp.tile(k_norm[l], kw // HEAD_DIM).reshape(1, kw)
        a_tab = _na_bias_tables(na_rpb[l], s_len // GRID_W)

        x = _ffn(x, mod3, row_x, 0, norm_ffn1[l], w1i, w1o, tm=tm)
        h = _ffn(h.reshape(1, bsz * c_len, d), mod3, row_h, 0, norm_ffn1[l], w1i, w1o,
                 tm=hb * c_len).reshape(bsz, c_len, d)

        fx, gq, gk, gv, nq, nk, nv, sgx = _inproj(
            x, mod3, row_x, norm_mix[l], w_proj, qg, kg, ones, rope, nb=1, tm=tm)
        fh, gqh, gkh, gvh, nqh, nkh, nvh, sgh = _inproj(
            h, mod3, row_h, norm_mix[l], w_proj, qg, kg, ones, None, nb=hb, tm=c_len)

        yf = _fourier(fx, dft_x)
        yg = _gqa(gq, [(gk, gv), (gkh, gvh)], tq=tq_att, sub=256)
        yn = _na(nq, nk, nv, nkh, nvh, a_tab, blocks_per_step=na_bps)
        x = _merge(x, mod3, row_x, yf, yg, yn, sgx, w_f, w_g, w_n, w_out, tm=tm)
        x = _ffn(x, mod3, row_x, 6, norm_ffn2[l], w2i, w2o, tm=tm, final_gain=final_norm if last else None)
        if not last:
            yfh = _fourier(fh, dft_h)
            ygh = _gqa(gqh, [(gkh, gvh)], tq=c_len, sub=c_len)
            ynh = _gqa(nqh, [(nkh, nvh)], tq=c_len, sub=c_len)
            h = _merge(h, mod3, row_h, yfh, ygh, ynh, sgh, w_f, w_g, w_n, w_out, tm=c_len)
            h = _ffn(h.reshape(1, bsz * c_len, d), mod3, row_h, 6, norm_ffn2[l], w2i, w2o,
                     tm=hb * c_len).reshape(bsz, c_len, d)
    return x
```

```python
import functools
import math

import numpy as np
import jax
import jax.numpy as jnp
from jax import lax
from jax.experimental import pallas as pl
from jax.experimental.pallas import tpu as pltpu

HEAD_DIM = 64
GRID_W = 64
FOURIER_GROUPS = 4
NA_WIN_H = 8
NA_WIN_W = 16
ROPE_THETA = 10000.0
N_MOD = 9
EPS = 1e-6
KV_GROUP = 4
DFT_TABLE_SPLIT = 64

LANES = 128
VMEM_PHYSICAL_BYTES = 64 * 1024 * 1024

NA_Q_ROWS = 4
NA_K_ROWS = NA_Q_ROWS + NA_WIN_H
NEG_BIG = -1e30
LOG2_E = math.log2(math.e)
SCORE_SCALE = HEAD_DIM ** -0.5 * LOG2_E

F32 = jnp.float32
BF16 = jnp.bfloat16


def _dot(a, b):
    return jnp.dot(a, b, preferred_element_type=F32)


def _dot_nt(a, b):
    return lax.dot_general(a, b, (((1,), (1,)), ((), ())), preferred_element_type=F32)


def _params(sem, vmem_bytes):
    return pltpu.CompilerParams(dimension_semantics=sem,
                                vmem_limit_bytes=int(min(vmem_bytes, VMEM_PHYSICAL_BYTES - (4 << 20))))


def _resident(shape, index_map):
    return pl.BlockSpec(shape, index_map, pipeline_mode=pl.Buffered(1))


def _layer_weight(w, l):
    return _resident((None,) + w.shape[1:], lambda *_: (l, 0, 0))


def _norm_mod(x, gain, shift, scale):
    y = x * lax.rsqrt(jnp.mean(x * x, axis=-1, keepdims=True) + EPS) * gain
    return y * (1.0 + scale) + shift


def _mod_kernel(c_ref, w_ref, b_ref, o_ref):
    c = c_ref[...]
    s = (c * jax.nn.sigmoid(c)).astype(BF16)
    o_ref[0] = _dot(s, w_ref[0].astype(BF16)) + b_ref[0]


def _modulation(cc, mod_w, mod_b, tn=1024):
    depth, d, n = mod_w.shape
    rows = cc.shape[0]
    return pl.pallas_call(
        _mod_kernel,
        grid=(depth, n // tn),
        in_specs=[pl.BlockSpec((rows, d), lambda l, j: (0, 0)),
                  pl.BlockSpec((1, d, tn), lambda l, j: (l, 0, j)),
                  pl.BlockSpec((1, 1, tn), lambda l, j: (l, 0, j))],
        out_specs=pl.BlockSpec((1, rows, tn), lambda l, j: (l, 0, j)),
        out_shape=jax.ShapeDtypeStruct((depth, rows, n), F32),
        compiler_params=_params(("parallel", "parallel"), 2 * (d * tn * 4) + 8 * d * tn),
        name="modulation",
    )(cc, mod_w, mod_b.reshape(depth, 1, n))


def _ffn_kernel(*refs, d_ff, tf, final):
    if final:
        x_ref, sh_ref, sc_ref, gt_ref, g_ref, win_ref, wout_ref, fg_ref, o_ref, u_scr, h_scr = refs
    else:
        x_ref, sh_ref, sc_ref, gt_ref, g_ref, win_ref, wout_ref, o_ref, u_scr, h_scr = refs
    x = x_ref[0]
    u_scr[...] = _norm_mod(x, g_ref[...], sh_ref[0], sc_ref[0]).astype(BF16)
    for j in range(d_ff // tf):
        u = u_scr[...]
        a = _dot(u, win_ref[:, j * tf:(j + 1) * tf])
        g = _dot(u, win_ref[:, d_ff + j * tf:d_ff + (j + 1) * tf])
        h_scr[:, j * tf:(j + 1) * tf] = (g * jax.nn.sigmoid(g) * a).astype(BF16)
    y = x + (0.5 * gt_ref[0]) * _dot(h_scr[...], wout_ref[...])
    if final:
        y = y * lax.rsqrt(jnp.mean(y * y, axis=-1, keepdims=True) + EPS) * fg_ref[...]
    o_ref[0] = y


def _ffn(x, mod3, row_fn, k0, gain, w_in, w_out, l, *, tm, final_gain=None, tf=256):
    nb, t, d = x.shape
    d_ff = w_out.shape[1]
    final = final_gain is not None
    row = lambda k: pl.BlockSpec((1, 1, d), lambda b, i: (row_fn(b, k), 0, 0))
    vec = pl.BlockSpec((1, d), lambda b, i: (0, 0))
    in_specs = [pl.BlockSpec((1, tm, d), lambda b, i: (b, i, 0)), row(k0), row(k0 + 1), row(k0 + 2), vec,
                _layer_weight(w_in, l), _layer_weight(w_out, l)]
    args = [x, mod3, mod3, mod3, gain.reshape(1, d), w_in, w_out]
    if final:
        in_specs.append(vec)
        args.append(final_gain.reshape(1, d))
    vmem = (3 * d * d_ff * 2
            + 4 * tm * d * 4
            + tm * d * 2 + tm * d_ff * 2
            + 2 * tm * d * 4 + 4 * tm * tf * 4)
    return pl.pallas_call(
        functools.partial(_ffn_kernel, d_ff=d_ff, tf=tf, final=final),
        grid=(nb, t // tm),
        in_specs=in_specs,
        out_specs=pl.BlockSpec((1, tm, d), lambda b, i: (b, i, 0)),
        out_shape=jax.ShapeDtypeStruct(x.shape, F32),
        scratch_shapes=[pltpu.VMEM((tm, d), BF16), pltpu.VMEM((tm, d_ff), BF16)],
        compiler_params=_params(("parallel", "parallel"), vmem + (8 << 20)),
        name="ffn",
    )(*args)


def _head_norm(x, gain, ones):
    ss = _dot((x * x).astype(BF16), ones)
    return x * lax.rsqrt(ss * (1.0 / HEAD_DIM) + EPS) * gain


def _rope(y, cos, ssin):
    even = lax.broadcasted_iota(jnp.int32, (1, LANES), 1) % 2 == 0
    cols = []
    for c in range(y.shape[1] // LANES):
        yc = y[:, c * LANES:(c + 1) * LANES]
        partner = jnp.where(even, pltpu.roll(yc, LANES - 1, 1), pltpu.roll(yc, 1, 1))
        cols.append(yc * cos + partner * ssin)
    return cols[0] if len(cols) == 1 else jnp.concatenate(cols, axis=1)


def _inproj_kernel(*refs, rope, splits, d):
    if rope:
        (x_ref, sh_ref, sc_ref, g_ref, w_ref, qg_ref, kg_ref, ones_ref, cos_ref, sin_ref,
         f_ref, q_ref, ka_ref, kb_ref, va_ref, vb_ref, nq_ref, nk_ref, nv_ref, sg_ref, u_scr) = refs
    else:
        (x_ref, sh_ref, sc_ref, g_ref, w_ref, qg_ref, kg_ref, ones_ref,
         f_ref, q_ref, ka_ref, kb_ref, va_ref, vb_ref, nq_ref, nk_ref, nv_ref, sg_ref, u_scr) = refs
    m = u_scr.shape[0]
    x = x_ref[...].reshape(m, d)
    u_scr[...] = _norm_mod(x, g_ref[...], sh_ref[0], sc_ref[0]).astype(BF16)
    s_f, s_q, s_k, s_v, s_nq, s_nk, s_nv = splits
    r = _dot(u_scr[...], w_ref[:, :s_nv])
    kw = s_k - s_q
    q = _head_norm(r[:, s_f:s_q], qg_ref[...], ones_ref[...])
    k = _head_norm(r[:, s_q:s_k], kg_ref[...], ones_ref[:kw, :kw])
    if rope:
        q = _rope(q, cos_ref[...], sin_ref[...])
        k = _rope(k, cos_ref[...], sin_ref[...])
    scale = SCORE_SCALE
    put = lambda ref, val: ref.__setitem__(Ellipsis, val.astype(BF16).reshape(ref.shape))
    put(f_ref, r[:, :s_f])
    put(q_ref, q * scale)
    lo = _lane_lo()
    for val, a_ref, b_ref in ((k, ka_ref, kb_ref), (r[:, s_k:s_v], va_ref, vb_ref)):
        swapped = pltpu.roll(val, HEAD_DIM, 1)
        put(a_ref, jnp.where(lo, val, swapped))
        put(b_ref, jnp.where(lo, swapped, val))
    put(nq_ref, r[:, s_v:s_nq] * scale)
    put(nk_ref, r[:, s_nq:s_nk])
    put(nv_ref, r[:, s_nk:s_nv])
    for c in range(3):
        g = _dot(u_scr[...], w_ref[:, s_nv + c * d:s_nv + (c + 1) * d])
        sg_ref[..., c * d:(c + 1) * d] = jax.nn.sigmoid(g).astype(BF16).reshape(sg_ref.shape[:-1] + (d,))


def _inproj(x, mod3, row_fn, gain, w, l, qg, kg, ones, tables, *, nb, tm):
    b_, t, d = x.shape
    n_tot = w.shape[2]
    fw, qw, kw, naw = d // 4, d // 2, d // 8, d // 4
    widths = [fw, qw, kw, kw, naw, naw, naw]
    splits = tuple(int(v) for v in np.cumsum(widths))
    rope = tables is not None
    row = lambda k: pl.BlockSpec((1, 1, d), lambda b, i: (row_fn(b, k), 0, 0))
    const = lambda shape: pl.BlockSpec(shape, lambda b, i: (0,) * len(shape))
    tok = lambda wd: pl.BlockSpec((nb, tm, wd), lambda b, i: (b, i, 0))
    in_specs = [tok(d), row(3), row(4), const((1, d)), _layer_weight(w, l),
                const((1, qw)), const((1, kw)), const((qw, qw))]
    args = [x, mod3, mod3, gain.reshape(1, d), w, qg, kg, ones]
    if rope:
        in_specs += [pl.BlockSpec((tm, LANES), lambda b, i: (i, 0))] * 2
        args += list(tables)
    out_widths = [fw, qw, kw, kw, kw, kw, naw, naw, naw, 3 * d]
    m = nb * tm
    vmem = (d * n_tot * 2 + 2 * m * d * 4 + m * d * 2 + 2 * m * n_tot * 2
            + m * splits[-1] * 4 + 6 * m * qw * 4 + 2 * m * d * 4)
    return pl.pallas_call(
        functools.partial(_inproj_kernel, rope=rope, splits=splits, d=d),
        grid=(b_ // nb, t // tm),
        in_specs=in_specs,
        out_specs=[tok(wd) for wd in out_widths],
        out_shape=[jax.ShapeDtypeStruct((b_, t, wd), BF16) for wd in out_widths],
        scratch_shapes=[pltpu.VMEM((m, d), BF16)],
        compiler_params=_params(("parallel", "parallel"), vmem + (8 << 20)),
        name="inproj",
    )(*args)


def _fourier_kernel(x_ref, cs_ref, ct_ref, st_ref, o_ref):
    w = x_ref.shape[2]
    ab = _dot(x_ref[0], cs_ref[...]).astype(BF16)
    y = _dot(ct_ref[...], ab[:, :w]) - _dot(st_ref[...], ab[:, w:])
    o_ref[0] = y.astype(BF16)


def _dft_tables(t, w):
    def angles(rows, cols, n):
        prod = (jnp.arange(rows, dtype=jnp.int32)[:, None] * jnp.arange(cols, dtype=jnp.int32)[None, :]) % n
        return prod.astype(F32) * (2.0 * math.pi / n)

    def cs(n):
        ang = angles(n, n, n)
        return jnp.cos(ang) * (1.0 / math.sqrt(n)), jnp.sin(ang) * (1.0 / math.sqrt(n))

    def cs_split(n, outer):
        inner = n // outer
        ang_a = angles(outer, n, outer)[:, None, :]
        ang_b = angles(inner, n, n)[None, :, :]
        ca, sa = jnp.cos(ang_a), jnp.sin(ang_a)
        cb, sb = jnp.cos(ang_b) * (1.0 / math.sqrt(n)), jnp.sin(ang_b) * (1.0 / math.sqrt(n))
        return (ca * cb - sa * sb).reshape(n, n), (sa * cb + ca * sb).reshape(n, n)

    ct, st = cs_split(t, DFT_TABLE_SPLIT) if t % DFT_TABLE_SPLIT == 0 and t > DFT_TABLE_SPLIT else cs(t)
    cg, sg = cs(w // FOURIER_GROUPS)
    eye = jnp.eye(FOURIER_GROUPS, dtype=F32)
    cs_c = jnp.concatenate([jnp.kron(eye, cg), jnp.kron(eye, sg)], axis=1)
    return cs_c.astype(BF16), ct.astype(BF16), st.astype(BF16)


def _fourier(f, tables):
    b_, t, w = f.shape
    cs_c, ct, st = tables
    return pl.pallas_call(
        _fourier_kernel,
        grid=(b_,),
        in_specs=[pl.BlockSpec((1, t, w), lambda b: (b, 0, 0)),
                  _resident((w, 2 * w), lambda b: (0, 0)),
                  _resident((t, t), lambda b: (0, 0)),
                  _resident((t, t), lambda b: (0, 0))],
        out_specs=pl.BlockSpec((1, t, w), lambda b: (b, 0, 0)),
        out_shape=jax.ShapeDtypeStruct(f.shape, BF16),
        compiler_params=_params(("parallel",), 2 * t * t * 2 + 4 * t * w * 2 + 4 * t * w * 4 + (8 << 20)),
        name="fourier",
    )(f, cs_c, ct, st)


def _lane_lo():
    return lax.broadcasted_iota(jnp.int32, (1, LANES), 1) < HEAD_DIM


def _stack_heads(qc, lo):
    zero = jnp.zeros_like(qc)
    return jnp.concatenate([jnp.where(lo, qc, zero), jnp.where(lo, zero, qc)], axis=0)


def _unstack_heads(o, lo):
    t = o.shape[0] // 2
    return jnp.where(lo, o[:t], o[t:])


def _attend(q2, blocks):
    scores = []
    for k, _, bias in blocks:
        s = _dot_nt(q2, k)
        scores.append(s if bias is None else s + bias)
    m = functools.reduce(jnp.maximum, [jnp.max(s, axis=-1, keepdims=True) for s in scores])
    o = functools.reduce(jnp.add, [_dot(jnp.exp2(s - m).astype(BF16), v) for s, (_, v, _) in zip(scores, blocks)])
    return o[:, :LANES] / o[:, LANES:]


def _with_ones(v):
    return jnp.concatenate([v, jnp.ones_like(v)], axis=1)


def _gqa_kernel(*refs, sub, tile_kv):
    q_ref, o_ref = refs[0], refs[-1]
    kv_refs = refs[1:-1]
    lo = _lane_lo()
    slabs = {}
    for entry in sorted({e for tile in tile_kv for e in tile}):
        ki, vi, col0 = entry
        cols = slice(col0, col0 + LANES)
        slabs[entry] = (kv_refs[ki][0, :, cols], _with_ones(kv_refs[vi][0, :, cols]), None)
    for r in range(q_ref.shape[1] // sub):
        rows = slice(r * sub, (r + 1) * sub)
        for c, tile in enumerate(tile_kv):
            cols = slice(c * LANES, (c + 1) * LANES)
            q2 = _stack_heads(q_ref[0, rows, cols], lo)
            o_ref[0, rows, cols] = _unstack_heads(_attend(q2, [slabs[e] for e in tile]), lo).astype(BF16)


def _gqa(q, kv_arrays, tile_kv, *, tq, sub):
    b_, t, w = q.shape
    assert len(tile_kv) == w // LANES
    in_specs = [pl.BlockSpec((1, tq, w), lambda b, i: (b, i, 0))]
    in_specs += [pl.BlockSpec((1,) + a.shape[1:], lambda b, i: (b, 0, 0)) for a in kv_arrays]
    args = [q] + list(kv_arrays)
    n_keys = max(sum(kv_arrays[ki].shape[1] for ki, _, _ in tile) for tile in tile_kv)
    vmem = (4 * tq * w * 2 + sum(2 * a.shape[1] * a.shape[2] * 2 for a in kv_arrays)
            + 3 * (2 * sub) * n_keys * (4 + 4 + 2) + (8 << 20))
    return pl.pallas_call(
        functools.partial(_gqa_kernel, sub=sub, tile_kv=tile_kv),
        grid=(b_, t // tq),
        in_specs=in_specs,
        out_specs=pl.BlockSpec((1, tq, w), lambda b, i: (b, i, 0)),
        out_shape=jax.ShapeDtypeStruct(q.shape, BF16),
        compiler_params=_params(("parallel", "parallel"), vmem),
        name="gqa",
    )(*args)


def _na_kernel(q_ref, k_ref, v_ref, kc_ref, vc_ref, a_ref, o_ref, *, rows):
    tq = NA_Q_ROWS * GRID_W
    per_step = q_ref.shape[1] // tq
    nblk = rows // NA_Q_ROWS
    lo = _lane_lo()
    for r in range(per_step):
        j = pl.program_id(1) * per_step + r
        sel = jnp.where(j == 0, 0, jnp.where(j == nblk - 1, 2, 1))
        ks = jnp.clip(j * NA_Q_ROWS - NA_WIN_H // 2, 0, rows - NA_K_ROWS)
        loc = pl.ds(pl.multiple_of(ks * GRID_W, NA_Q_ROWS * GRID_W), NA_K_ROWS * GRID_W)
        q_rows = slice(r * tq, (r + 1) * tq)
        for c in range(q_ref.shape[2] // LANES):
            cols = slice(c * LANES, (c + 1) * LANES)
            q2 = _stack_heads(q_ref[0, q_rows, cols], lo)
            blocks = [(k_ref[0, loc, cols], _with_ones(v_ref[0, loc, cols]), a_ref[sel, c]),
                      (kc_ref[0, :, cols], _with_ones(vc_ref[0, :, cols]), None)]
            o_ref[0, q_rows, cols] = _unstack_heads(_attend(q2, blocks), lo).astype(BF16)


def _na_bias_tables(rpb, rows):
    nh, n_dr, n_dc = rpb.shape
    tq = NA_Q_ROWS * GRID_W
    return pl.pallas_call(
        functools.partial(_na_bias_kernel, rows=rows, nh=nh, n_dr=n_dr, n_dc=n_dc),
        in_specs=[pl.BlockSpec(memory_space=pltpu.SMEM)],
        out_shape=jax.ShapeDtypeStruct((3, nh // 2, 2 * tq, NA_K_ROWS * GRID_W), F32),
        compiler_params=pltpu.CompilerParams(
            vmem_limit_bytes=2 * 3 * nh * tq * NA_K_ROWS * GRID_W * 4 + (8 << 20)),
        name="na_bias",
    )(rpb.reshape(-1))


def _na_bias_kernel(rpb_ref, o_ref, *, rows, nh, n_dr, n_dc):
    kh = min(NA_WIN_H, rows)
    qc = lax.broadcasted_iota(jnp.int32, (GRID_W, GRID_W), 0)
    kc = lax.broadcasted_iota(jnp.int32, (GRID_W, GRID_W), 1)
    cs = jnp.clip(qc - NA_WIN_W // 2, 0, GRID_W - NA_WIN_W)
    band = (kc >= cs) & (kc < cs + NA_WIN_W)
    dc = kc - qc + (NA_WIN_W - 1)
    o_ref[...] = jnp.full(o_ref.shape, NEG_BIG, F32)
    uses = {}
    for sel, r0 in enumerate((0, NA_Q_ROWS, rows - NA_Q_ROWS)):
        ks = int(np.clip(r0 - NA_WIN_H // 2, 0, rows - NA_K_ROWS))
        for qi in range(NA_Q_ROWS):
            r = r0 + qi
            rs = int(np.clip(r - kh // 2, 0, rows - kh))
            for ki in range(NA_K_ROWS):
                kr = ks + ki
                if rs <= kr < rs + kh:
                    uses.setdefault(kr - r + NA_WIN_H - 1, []).append((sel, qi, ki))
    for h in range(nh):
        c, e = divmod(h, 2)
        for dr, dests in sorted(uses.items()):
            tile = jnp.full((GRID_W, GRID_W), NEG_BIG, F32)
            for j in range(n_dc):
                tile = jnp.where(band & (dc == j), rpb_ref[(h * n_dr + dr) * n_dc + j] * LOG2_E, tile)
            for sel, qi, ki in dests:
                r_lo = (e * NA_Q_ROWS + qi) * GRID_W
                o_ref[sel, c, r_lo:r_lo + GRID_W, ki * GRID_W:(ki + 1) * GRID_W] = tile


def _na(q, k, v, kc, vc, a_tab, *, blocks_per_step):
    b_, t, w = q.shape
    blk = NA_Q_ROWS * GRID_W
    tq = blocks_per_step * blk
    n_loc = NA_K_ROWS * GRID_W
    full = lambda a: pl.BlockSpec((1,) + a.shape[1:], lambda b, i: (b, 0, 0))
    vmem = (a_tab.size * 4 + 4 * tq * w * 2 + 4 * (k.shape[1] + kc.shape[1]) * w * 2
            + 3 * (2 * blk) * (n_loc + kc.shape[1]) * (4 + 4 + 2) + (8 << 20))
    return pl.pallas_call(
        functools.partial(_na_kernel, rows=t // GRID_W),
        grid=(b_, t // tq),
        in_specs=[pl.BlockSpec((1, tq, w), lambda b, i: (b, i, 0)), full(k), full(v), full(kc), full(vc),
                  _resident(a_tab.shape, lambda b, i: (0, 0, 0, 0))],
        out_specs=pl.BlockSpec((1, tq, w), lambda b, i: (b, i, 0)),
        out_shape=jax.ShapeDtypeStruct(q.shape, BF16),
        compiler_params=_params(("parallel", "arbitrary"), vmem),
        name="na",
    )(q, k, v, kc, vc, a_tab)


def _merge_kernel(x_ref, gt_ref, yf_ref, yg_ref, yn_ref, sg_ref, wf_ref, wg_ref, wn_ref, wo_ref, o_ref):
    d = x_ref.shape[2]
    m = (sg_ref[0, :, :d].astype(F32) * _dot(yf_ref[0], wf_ref[...])
         + sg_ref[0, :, d:2 * d].astype(F32) * _dot(yg_ref[0], wg_ref[...])
         + sg_ref[0, :, 2 * d:].astype(F32) * _dot(yn_ref[0], wn_ref[...]))
    o_ref[0] = x_ref[0] + gt_ref[0] * _dot(m.astype(BF16), wo_ref[...])


def _merge(x, mod3, row_fn, yf, yg, yn, sg, wf, wg, wn, wo, l, *, tm):
    b_, t, d = x.shape
    tok = lambda wd: pl.BlockSpec((1, tm, wd), lambda b, i: (b, i, 0))
    weights = (wf, wg, wn, wo)
    vmem = (sum(2 * a.shape[1] * a.shape[2] for a in weights) + 4 * tm * d * 4
            + 2 * tm * (yf.shape[2] + yg.shape[2] + yn.shape[2] + 3 * d) * 2 + 6 * tm * d * 4 + (8 << 20))
    return pl.pallas_call(
        _merge_kernel,
        grid=(b_, t // tm),
        in_specs=[tok(d), pl.BlockSpec((1, 1, d), lambda b, i: (row_fn(b, 5), 0, 0)),
                  tok(yf.shape[2]), tok(yg.shape[2]), tok(yn.shape[2]), tok(3 * d)]
                 + [_layer_weight(a, l) for a in weights],
        out_specs=tok(d),
        out_shape=jax.ShapeDtypeStruct(x.shape, F32),
        compiler_params=_params(("parallel", "parallel"), vmem),
        name="merge",
    )(x, mod3, yf, yg, yn, sg, *weights)


def _rope_tables(seq_len):
    t = np.arange(seq_len)
    pairs = HEAD_DIM // 4
    freqs = jnp.asarray(ROPE_THETA, F32) ** (-jnp.arange(pairs, dtype=F32) / pairs)
    row = jnp.asarray(t // GRID_W, F32)
    col = jnp.asarray(t % GRID_W, F32)
    ang = jnp.concatenate([row[:, None] * freqs, col[:, None] * freqs], axis=-1)
    cos = jnp.repeat(jnp.cos(ang), 2, axis=-1)
    sin = jnp.repeat(jnp.sin(ang), 2, axis=-1) * jnp.asarray(np.tile([-1.0, 1.0], HEAD_DIM // 2), F32)
    reps = LANES // HEAD_DIM
    return jnp.tile(cos, (1, reps)), jnp.tile(sin, (1, reps))


def kernel(x, c, ctx, c_ctx, mod_w, mod_b, norm_ffn1, ffn1_w_in, ffn1_w_out, norm_mix, w_in, q_norm, k_norm,
           na_rpb, w_fourier, w_gqa_out, w_na_out, w_o, norm_ffn2, ffn2_w_in, ffn2_w_out, final_norm):
    bsz, s_len, d = x.shape
    c_len = ctx.shape[1]
    depth = mod_w.shape[0]
    fw, qw, kw, naw = d // 4, d // 2, d // 8, d // 4
    n_q_heads = qw // HEAD_DIM
    assert kw == LANES and n_q_heads // (kw // HEAD_DIM) == KV_GROUP
    assert s_len % (NA_Q_ROWS * GRID_W) == 0 and s_len // GRID_W >= NA_K_ROWS
    ctx_rows = 32
    assert bsz < ctx_rows

    cc = jnp.zeros((ctx_rows, d), F32).at[:bsz].set(c).at[bsz].set(c_ctx)
    mod3 = _modulation(cc, mod_w, mod_b).reshape(depth * ctx_rows * N_MOD, 1, d)

    rope = _rope_tables(s_len)
    dft_x = _dft_tables(s_len, fw)
    dft_h = _dft_tables(c_len, fw)
    ones = jnp.asarray(np.kron(np.eye(n_q_heads), np.ones((HEAD_DIM, HEAD_DIM))), BF16)
    bf = lambda a: a.astype(BF16)
    w1i, w1o, w2i, w2o, w_proj = bf(ffn1_w_in), bf(ffn1_w_out), bf(ffn2_w_in), bf(ffn2_w_out), bf(w_in)
    w_f, w_g, w_n, w_out = bf(w_fourier), bf(w_gqa_out), bf(w_na_out), bf(w_o)

    q_tiles = qw // LANES
    per_group = q_tiles // (kw // HEAD_DIM)
    gqa_x = tuple(((2 * (c // per_group), 2 * (c // per_group) + 1, 0),
                   (4 + 2 * (c // per_group), 5 + 2 * (c // per_group), 0)) for c in range(q_tiles))
    gqa_h = tuple(((2 * (c // per_group), 2 * (c // per_group) + 1, 0),) for c in range(q_tiles))
    na_h = tuple(((0, 1, c * LANES),) for c in range(naw // LANES))

    tm = 512 if s_len % 512 == 0 else 256
    tq_att = 1024 if s_len % 1024 == 0 else 256
    na_bps = 2 if s_len % (2 * NA_Q_ROWS * GRID_W) == 0 else 1
    h = ctx
    hb = 4 if bsz % 4 == 0 else 1
    for l in range(depth):
        last = l == depth - 1
        row_x = lambda b, k, l=l: (l * ctx_rows + b) * N_MOD + k
        row_h = lambda b, k, l=l: (l * ctx_rows + bsz) * N_MOD + k
        qg = jnp.tile(q_norm[l], n_q_heads).reshape(1, qw)
        kg = jnp.tile(k_norm[l], kw // HEAD_DIM).reshape(1, kw)
        a_tab = _na_bias_tables(na_rpb[l], s_len // GRID_W)

        x = _ffn(x, mod3, row_x, 0, norm_ffn1[l], w1i, w1o, l, tm=tm)
        h = _ffn(h.reshape(1, bsz * c_len, d), mod3, row_h, 0, norm_ffn1[l], w1i, w1o, l,
                 tm=hb * c_len).reshape(bsz, c_len, d)

        fx, gq, ka, kb, va, vb, nq, nk, nv, sgx = _inproj(
            x, mod3, row_x, norm_mix[l], w_proj, l, qg, kg, ones, rope, nb=1, tm=tm)
        fh, gqh, kah, kbh, vah, vbh, nqh, nkh, nvh, sgh = _inproj(
            h, mod3, row_h, norm_mix[l], w_proj, l, qg, kg, ones, None, nb=hb, tm=c_len)

        yf = _fourier(fx, dft_x)
        yg = _gqa(gq, [ka, va, kb, vb, kah, vah, kbh, vbh], gqa_x, tq=tq_att, sub=256)
        yn = _na(nq, nk, nv, nkh, nvh, a_tab, blocks_per_step=na_bps)
        x = _merge(x, mod3, row_x, yf, yg, yn, sgx, w_f, w_g, w_n, w_out, l, tm=tm)
        x = _ffn(x, mod3, row_x, 6, norm_ffn2[l], w2i, w2o, l, tm=tm, final_gain=final_norm if last else None)
        if not last:
            yfh = _fourier(fh, dft_h)
            ygh = _gqa(gqh, [kah, vah, kbh, vbh], gqa_h, tq=c_len, sub=c_len)
            ynh = _gqa(nqh, [nkh, nvh], na_h, tq=c_len, sub=c_len)
            h = _merge(h, mod3, row_h, yfh, ygh, ynh, sgh, w_f, w_g, w_n, w_out, l, tm=c_len)
            h = _ffn(h.reshape(1, bsz * c_len, d), mod3, row_h, 6, norm_ffn2[l], w2i, w2o, l,
                     tm=hb * c_len).reshape(bsz, c_len, d)
    return x
```

```python
import functools
import math

import numpy as np
import jax
import jax.numpy as jnp
from jax import lax
from jax.experimental import pallas as pl
from jax.experimental.pallas import tpu as pltpu

HEAD_DIM = 64
GRID_W = 64
FOURIER_GROUPS = 4
NA_WIN_H = 8
NA_WIN_W = 16
ROPE_THETA = 10000.0
N_MOD = 9
EPS = 1e-6
KV_GROUP = 4
DFT_TABLE_SPLIT = 64

LANES = 128
VMEM_PHYSICAL_BYTES = 64 * 1024 * 1024

NA_Q_ROWS = 4
NA_K_ROWS = NA_Q_ROWS + NA_WIN_H
NEG_BIG = -1e30
LOG2_E = math.log2(math.e)
SCORE_SCALE = HEAD_DIM ** -0.5 * LOG2_E

F32 = jnp.float32
BF16 = jnp.bfloat16


def _dot(a, b):
    return jnp.dot(a, b, preferred_element_type=F32)


def _dot_nt(a, b):
    return lax.dot_general(a, b, (((1,), (1,)), ((), ())), preferred_element_type=F32)


def _params(sem, vmem_bytes):
    return pltpu.CompilerParams(dimension_semantics=sem,
                                vmem_limit_bytes=int(min(vmem_bytes, VMEM_PHYSICAL_BYTES - (4 << 20))))


def _resident(shape, index_map):
    return pl.BlockSpec(shape, index_map, pipeline_mode=pl.Buffered(1))


def _layer_weight(w, l):
    return _resident((None,) + w.shape[1:], lambda *_: (l, 0, 0))


def _norm_mod(x, gain, shift, scale):
    y = x * lax.rsqrt(jnp.mean(x * x, axis=-1, keepdims=True) + EPS)
    return y * (gain * (1.0 + scale)) + shift


def _mod_kernel(c_ref, w_ref, b_ref, o_ref):
    c = c_ref[...]
    s = (c * jax.nn.sigmoid(c)).astype(BF16)
    o_ref[0] = _dot(s, w_ref[0].astype(BF16)) + b_ref[0]


def _modulation(cc, mod_w, mod_b, tn=1024):
    depth, d, n = mod_w.shape
    rows = cc.shape[0]
    return pl.pallas_call(
        _mod_kernel,
        grid=(depth, n // tn),
        in_specs=[pl.BlockSpec((rows, d), lambda l, j: (0, 0)),
                  pl.BlockSpec((1, d, tn), lambda l, j: (l, 0, j)),
                  pl.BlockSpec((1, 1, tn), lambda l, j: (l, 0, j))],
        out_specs=pl.BlockSpec((1, rows, tn), lambda l, j: (l, 0, j)),
        out_shape=jax.ShapeDtypeStruct((depth, rows, n), F32),
        compiler_params=_params(("parallel", "parallel"), 2 * (d * tn * 4) + 8 * d * tn),
        name="modulation",
    )(cc, mod_w, mod_b.reshape(depth, 1, n))


def _ffn_kernel(*refs, d_ff, tf, final):
    if final:
        x_ref, sh_ref, sc_ref, gt_ref, g_ref, win_ref, wout_ref, fg_ref, o_ref, u_scr, h_scr = refs
    else:
        x_ref, sh_ref, sc_ref, gt_ref, g_ref, win_ref, wout_ref, o_ref, u_scr, h_scr = refs
    x = x_ref[0]
    u_scr[...] = _norm_mod(x, g_ref[...], sh_ref[0], sc_ref[0]).astype(BF16)
    for j in range(d_ff // tf):
        u = u_scr[...]
        a = _dot(u, win_ref[:, j * tf:(j + 1) * tf])
        g = _dot(u, win_ref[:, d_ff + j * tf:d_ff + (j + 1) * tf])
        h_scr[:, j * tf:(j + 1) * tf] = (g * jax.nn.sigmoid(g) * a).astype(BF16)
    y = x + (0.5 * gt_ref[0]) * _dot(h_scr[...], wout_ref[...])
    if final:
        y = y * lax.rsqrt(jnp.mean(y * y, axis=-1, keepdims=True) + EPS) * fg_ref[...]
    o_ref[0] = y


def _ffn(x, mod3, row_fn, k0, gain, w_in, w_out, l, *, tm, final_gain=None, tf=256):
    nb, t, d = x.shape
    d_ff = w_out.shape[1]
    final = final_gain is not None
    row = lambda k: pl.BlockSpec((1, 1, d), lambda b, i: (row_fn(b, k), 0, 0))
    vec = pl.BlockSpec((1, d), lambda b, i: (0, 0))
    in_specs = [pl.BlockSpec((1, tm, d), lambda b, i: (b, i, 0)), row(k0), row(k0 + 1), row(k0 + 2), vec,
                _layer_weight(w_in, l), _layer_weight(w_out, l)]
    args = [x, mod3, mod3, mod3, gain.reshape(1, d), w_in, w_out]
    if final:
        in_specs.append(vec)
        args.append(final_gain.reshape(1, d))
    vmem = (3 * d * d_ff * 2
            + 4 * tm * d * 4
            + tm * d * 2 + tm * d_ff * 2
            + 2 * tm * d * 4 + 4 * tm * tf * 4)
    return pl.pallas_call(
        functools.partial(_ffn_kernel, d_ff=d_ff, tf=tf, final=final),
        grid=(nb, t // tm),
        in_specs=in_specs,
        out_specs=pl.BlockSpec((1, tm, d), lambda b, i: (b, i, 0)),
        out_shape=jax.ShapeDtypeStruct(x.shape, F32),
        scratch_shapes=[pltpu.VMEM((tm, d), BF16), pltpu.VMEM((tm, d_ff), BF16)],
        compiler_params=_params(("parallel", "parallel"), vmem + (8 << 20)),
        name="ffn",
    )(*args)


def _head_norm(x, gain, ones):
    ss = _dot((x * x).astype(BF16), ones)
    return x * lax.rsqrt(ss * (1.0 / HEAD_DIM) + EPS) * gain


def _rope(y, cos, ssin):
    even = lax.broadcasted_iota(jnp.int32, (1, LANES), 1) % 2 == 0
    cols = []
    for c in range(y.shape[1] // LANES):
        yc = y[:, c * LANES:(c + 1) * LANES]
        partner = jnp.where(even, pltpu.roll(yc, LANES - 1, 1), pltpu.roll(yc, 1, 1))
        cols.append(yc * cos + partner * ssin)
    return cols[0] if len(cols) == 1 else jnp.concatenate(cols, axis=1)


def _inproj_kernel(*refs, rope, splits, d):
    if rope:
        (x_ref, sh_ref, sc_ref, g_ref, w_ref, qg_ref, kg_ref, ones_ref, cos_ref, sin_ref,
         f_ref, q_ref, ka_ref, kb_ref, va_ref, vb_ref, nq_ref, nk_ref, nv_ref, sg_ref, u_scr) = refs
    else:
        (x_ref, sh_ref, sc_ref, g_ref, w_ref, qg_ref, kg_ref, ones_ref,
         f_ref, q_ref, ka_ref, kb_ref, va_ref, vb_ref, nq_ref, nk_ref, nv_ref, sg_ref, u_scr) = refs
    m = u_scr.shape[0]
    x = x_ref[...].reshape(m, d)
    u_scr[...] = _norm_mod(x, g_ref[...], sh_ref[0], sc_ref[0]).astype(BF16)
    s_f, s_q, s_k, s_v, s_nq, s_nk, s_nv = splits
    r = _dot(u_scr[...], w_ref[:, :s_nv])
    kw = s_k - s_q
    q = _head_norm(r[:, s_f:s_q], qg_ref[...], ones_ref[...])
    k = _head_norm(r[:, s_q:s_k], kg_ref[...], ones_ref[:kw, :kw])
    if rope:
        q = _rope(q, cos_ref[...], sin_ref[...])
        k = _rope(k, cos_ref[...], sin_ref[...])
    scale = SCORE_SCALE
    put = lambda ref, val: ref.__setitem__(Ellipsis, val.astype(BF16).reshape(ref.shape))
    put(f_ref, r[:, :s_f])
    put(q_ref, q * scale)
    lo = _lane_lo()
    for val, a_ref, b_ref in ((k, ka_ref, kb_ref), (r[:, s_k:s_v], va_ref, vb_ref)):
        swapped = pltpu.roll(val, HEAD_DIM, 1)
        put(a_ref, jnp.where(lo, val, swapped))
        put(b_ref, jnp.where(lo, swapped, val))
    put(nq_ref, r[:, s_v:s_nq] * scale)
    put(nk_ref, r[:, s_nq:s_nk])
    put(nv_ref, r[:, s_nk:s_nv])
    for c in range(3):
        g = _dot(u_scr[...], w_ref[:, s_nv + c * d:s_nv + (c + 1) * d])
        sg_ref[..., c * d:(c + 1) * d] = jax.nn.sigmoid(g).astype(BF16).reshape(sg_ref.shape[:-1] + (d,))


def _inproj(x, mod3, row_fn, gain, w, l, qg, kg, ones, tables, *, nb, tm):
    b_, t, d = x.shape
    n_tot = w.shape[2]
    fw, qw, kw, naw = d // 4, d // 2, d // 8, d // 4
    widths = [fw, qw, kw, kw, naw, naw, naw]
    splits = tuple(int(v) for v in np.cumsum(widths))
    rope = tables is not None
    row = lambda k: pl.BlockSpec((1, 1, d), lambda b, i: (row_fn(b, k), 0, 0))
    const = lambda shape: pl.BlockSpec(shape, lambda b, i: (0,) * len(shape))
    tok = lambda wd: pl.BlockSpec((nb, tm, wd), lambda b, i: (b, i, 0))
    in_specs = [tok(d), row(3), row(4), const((1, d)), _layer_weight(w, l),
                const((1, qw)), const((1, kw)), const((qw, qw))]
    args = [x, mod3, mod3, gain.reshape(1, d), w, qg, kg, ones]
    if rope:
        in_specs += [pl.BlockSpec((tm, LANES), lambda b, i: (i, 0))] * 2
        args += list(tables)
    out_widths = [fw, qw, kw, kw, kw, kw, naw, naw, naw, 3 * d]
    m = nb * tm
    vmem = (d * n_tot * 2 + 2 * m * d * 4 + m * d * 2 + 2 * m * n_tot * 2
            + m * splits[-1] * 4 + 6 * m * qw * 4 + 2 * m * d * 4)
    return pl.pallas_call(
        functools.partial(_inproj_kernel, rope=rope, splits=splits, d=d),
        grid=(b_ // nb, t // tm),
        in_specs=in_specs,
        out_specs=[tok(wd) for wd in out_widths],
        out_shape=[jax.ShapeDtypeStruct((b_, t, wd), BF16) for wd in out_widths],
        scratch_shapes=[pltpu.VMEM((m, d), BF16)],
        compiler_params=_params(("parallel", "parallel"), vmem + (8 << 20)),
        name="inproj",
    )(*args)


def _fourier_kernel(x_ref, cs_ref, ct_ref, st_ref, o_ref):
    w = x_ref.shape[2]
    ab = _dot(x_ref[0], cs_ref[...]).astype(BF16)
    y = _dot(ct_ref[...], ab[:, :w]) - _dot(st_ref[...], ab[:, w:])
    o_ref[0] = y.astype(BF16)


def _dft_tables(t, w):
    def angles(rows, cols, n):
        prod = (jnp.arange(rows, dtype=jnp.int32)[:, None] * jnp.arange(cols, dtype=jnp.int32)[None, :]) % n
        return prod.astype(F32) * (2.0 * math.pi / n)

    def cs(n):
        ang = angles(n, n, n)
        return jnp.cos(ang) * (1.0 / math.sqrt(n)), jnp.sin(ang) * (1.0 / math.sqrt(n))

    def cs_split(n, outer):
        inner = n // outer
        ang_a = angles(outer, n, outer)[:, None, :]
        ang_b = angles(inner, n, n)[None, :, :]
        ca, sa = jnp.cos(ang_a), jnp.sin(ang_a)
        cb, sb = jnp.cos(ang_b) * (1.0 / math.sqrt(n)), jnp.sin(ang_b) * (1.0 / math.sqrt(n))
        return (ca * cb - sa * sb).reshape(n, n), (sa * cb + ca * sb).reshape(n, n)

    ct, st = cs_split(t, DFT_TABLE_SPLIT) if t % DFT_TABLE_SPLIT == 0 and t > DFT_TABLE_SPLIT else cs(t)
    cg, sg = cs(w // FOURIER_GROUPS)
    eye = jnp.eye(FOURIER_GROUPS, dtype=F32)
    cs_c = jnp.concatenate([jnp.kron(eye, cg), jnp.kron(eye, sg)], axis=1)
    return cs_c.astype(BF16), ct.astype(BF16), st.astype(BF16)


def _fourier(f, tables):
    b_, t, w = f.shape
    cs_c, ct, st = tables
    return pl.pallas_call(
        _fourier_kernel,
        grid=(b_,),
        in_specs=[pl.BlockSpec((1, t, w), lambda b: (b, 0, 0)),
                  _resident((w, 2 * w), lambda b: (0, 0)),
                  _resident((t, t), lambda b: (0, 0)),
                  _resident((t, t), lambda b: (0, 0))],
        out_specs=pl.BlockSpec((1, t, w), lambda b: (b, 0, 0)),
        out_shape=jax.ShapeDtypeStruct(f.shape, BF16),
        compiler_params=_params(("parallel",), 2 * t * t * 2 + 4 * t * w * 2 + 4 * t * w * 4 + (8 << 20)),
        name="fourier",
    )(f, cs_c, ct, st)


def _lane_lo():
    return lax.broadcasted_iota(jnp.int32, (1, LANES), 1) < HEAD_DIM


def _stack_heads(qc, lo):
    zero = jnp.zeros_like(qc)
    return jnp.concatenate([jnp.where(lo, qc, zero), jnp.where(lo, zero, qc)], axis=0)


def _unstack_heads(o, lo):
    t = o.shape[0] // 2
    return jnp.where(lo, o[:t], o[t:])


def _attend(q2, blocks):
    scores = []
    for k, _, bias in blocks:
        s = _dot_nt(q2, k)
        scores.append(s if bias is None else s + bias)
    m = functools.reduce(jnp.maximum, [jnp.max(s, axis=-1, keepdims=True) for s in scores])
    o = functools.reduce(jnp.add, [_dot(jnp.exp2(s - m).astype(BF16), v) for s, (_, v, _) in zip(scores, blocks)])
    return o[:, :LANES] / o[:, LANES:]


def _with_ones(v):
    return jnp.concatenate([v, jnp.ones_like(v)], axis=1)


def _gqa_kernel(*refs, sub, tile_kv):
    q_ref, o_ref = refs[0], refs[-1]
    kv_refs = refs[1:-1]
    lo = _lane_lo()
    slabs = {}
    for entry in sorted({e for tile in tile_kv for e in tile}):
        ki, vi, col0 = entry
        cols = slice(col0, col0 + LANES)
        slabs[entry] = (kv_refs[ki][0, :, cols], _with_ones(kv_refs[vi][0, :, cols]), None)
    for r in range(q_ref.shape[1] // sub):
        rows = slice(r * sub, (r + 1) * sub)
        for c, tile in enumerate(tile_kv):
            cols = slice(c * LANES, (c + 1) * LANES)
            q2 = _stack_heads(q_ref[0, rows, cols], lo)
            o_ref[0, rows, cols] = _unstack_heads(_attend(q2, [slabs[e] for e in tile]), lo).astype(BF16)


def _gqa(q, kv_arrays, tile_kv, *, tq, sub):
    b_, t, w = q.shape
    assert len(tile_kv) == w // LANES
    in_specs = [pl.BlockSpec((1, tq, w), lambda b, i: (b, i, 0))]
    in_specs += [pl.BlockSpec((1,) + a.shape[1:], lambda b, i: (b, 0, 0)) for a in kv_arrays]
    args = [q] + list(kv_arrays)
    n_keys = max(sum(kv_arrays[ki].shape[1] for ki, _, _ in tile) for tile in tile_kv)
    vmem = (4 * tq * w * 2 + sum(2 * a.shape[1] * a.shape[2] * 2 for a in kv_arrays)
            + 3 * (2 * sub) * n_keys * (4 + 4 + 2) + (8 << 20))
    return pl.pallas_call(
        functools.partial(_gqa_kernel, sub=sub, tile_kv=tile_kv),
        grid=(b_, t // tq),
        in_specs=in_specs,
        out_specs=pl.BlockSpec((1, tq, w), lambda b, i: (b, i, 0)),
        out_shape=jax.ShapeDtypeStruct(q.shape, BF16),
        compiler_params=_params(("parallel", "parallel"), vmem),
        name="gqa",
    )(*args)


def _na_kernel(q_ref, k_ref, v_ref, kc_ref, vc_ref, a_ref, o_ref, *, rows):
    tq = NA_Q_ROWS * GRID_W
    per_step = q_ref.shape[1] // tq
    nblk = rows // NA_Q_ROWS
    lo = _lane_lo()
    for r in range(per_step):
        j = pl.program_id(1) * per_step + r
        sel = jnp.where(j == 0, 0, jnp.where(j == nblk - 1, 2, 1))
        ks = jnp.clip(j * NA_Q_ROWS - NA_WIN_H // 2, 0, rows - NA_K_ROWS)
        loc = pl.ds(pl.multiple_of(ks * GRID_W, NA_Q_ROWS * GRID_W), NA_K_ROWS * GRID_W)
        q_rows = slice(r * tq, (r + 1) * tq)
        for c in range(q_ref.shape[2] // LANES):
            cols = slice(c * LANES, (c + 1) * LANES)
            q2 = _stack_heads(q_ref[0, q_rows, cols], lo)
            blocks = [(k_ref[0, loc, cols], _with_ones(v_ref[0, loc, cols]), a_ref[sel, c]),
                      (kc_ref[0, :, cols], _with_ones(vc_ref[0, :, cols]), None)]
            o_ref[0, q_rows, cols] = _unstack_heads(_attend(q2, blocks), lo).astype(BF16)


def _na_bias_tables(rpb, rows):
    nh, n_dr, n_dc = rpb.shape
    tq = NA_Q_ROWS * GRID_W
    return pl.pallas_call(
        functools.partial(_na_bias_kernel, rows=rows, nh=nh, n_dr=n_dr, n_dc=n_dc),
        in_specs=[pl.BlockSpec(memory_space=pltpu.SMEM)],
        out_shape=jax.ShapeDtypeStruct((3, nh // 2, 2 * tq, NA_K_ROWS * GRID_W), F32),
        compiler_params=pltpu.CompilerParams(
            vmem_limit_bytes=2 * 3 * nh * tq * NA_K_ROWS * GRID_W * 4 + (8 << 20)),
        name="na_bias",
    )(rpb.reshape(-1))


def _na_bias_kernel(rpb_ref, o_ref, *, rows, nh, n_dr, n_dc):
    kh = min(NA_WIN_H, rows)
    qc = lax.broadcasted_iota(jnp.int32, (GRID_W, GRID_W), 0)
    kc = lax.broadcasted_iota(jnp.int32, (GRID_W, GRID_W), 1)
    cs = jnp.clip(qc - NA_WIN_W // 2, 0, GRID_W - NA_WIN_W)
    band = (kc >= cs) & (kc < cs + NA_WIN_W)
    dc = kc - qc + (NA_WIN_W - 1)
    o_ref[...] = jnp.full(o_ref.shape, NEG_BIG, F32)
    uses = {}
    for sel, r0 in enumerate((0, NA_Q_ROWS, rows - NA_Q_ROWS)):
        ks = int(np.clip(r0 - NA_WIN_H // 2, 0, rows - NA_K_ROWS))
        for qi in range(NA_Q_ROWS):
            r = r0 + qi
            rs = int(np.clip(r - kh // 2, 0, rows - kh))
            for ki in range(NA_K_ROWS):
                kr = ks + ki
                if rs <= kr < rs + kh:
                    uses.setdefault(kr - r + NA_WIN_H - 1, []).append((sel, qi, ki))
    for h in range(nh):
        c, e = divmod(h, 2)
        for dr, dests in sorted(uses.items()):
            tile = jnp.full((GRID_W, GRID_W), NEG_BIG, F32)
            for j in range(n_dc):
                tile = jnp.where(band & (dc == j), rpb_ref[(h * n_dr + dr) * n_dc + j] * LOG2_E, tile)
            for sel, qi, ki in dests:
                r_lo = (e * NA_Q_ROWS + qi) * GRID_W
                o_ref[sel, c, r_lo:r_lo + GRID_W, ki * GRID_W:(ki + 1) * GRID_W] = tile


def _na(q, k, v, kc, vc, a_tab, *, blocks_per_step):
    b_, t, w = q.shape
    blk = NA_Q_ROWS * GRID_W
    tq = blocks_per_step * blk
    n_loc = NA_K_ROWS * GRID_W
    full = lambda a: pl.BlockSpec((1,) + a.shape[1:], lambda b, i: (b, 0, 0))
    vmem = (a_tab.size * 4 + 4 * tq * w * 2 + 4 * (k.shape[1] + kc.shape[1]) * w * 2
            + 3 * (2 * blk) * (n_loc + kc.shape[1]) * (4 + 4 + 2) + (8 << 20))
    return pl.pallas_call(
        functools.partial(_na_kernel, rows=t // GRID_W),
        grid=(b_, t // tq),
        in_specs=[pl.BlockSpec((1, tq, w), lambda b, i: (b, i, 0)), full(k), full(v), full(kc), full(vc),
                  _resident(a_tab.shape, lambda b, i: (0, 0, 0, 0))],
        out_specs=pl.BlockSpec((1, tq, w), lambda b, i: (b, i, 0)),
        out_shape=jax.ShapeDtypeStruct(q.shape, BF16),
        compiler_params=_params(("parallel", "arbitrary"), vmem),
        name="na",
    )(q, k, v, kc, vc, a_tab)


def _merge_kernel(x_ref, gt_ref, yf_ref, yg_ref, yn_ref, sg_ref, wf_ref, wg_ref, wn_ref, wo_ref, o_ref):
    d = x_ref.shape[2]
    m = (sg_ref[0, :, :d].astype(F32) * _dot(yf_ref[0], wf_ref[...])
         + sg_ref[0, :, d:2 * d].astype(F32) * _dot(yg_ref[0], wg_ref[...])
         + sg_ref[0, :, 2 * d:].astype(F32) * _dot(yn_ref[0], wn_ref[...]))
    o_ref[0] = x_ref[0] + gt_ref[0] * _dot(m.astype(BF16), wo_ref[...])


def _merge(x, mod3, row_fn, yf, yg, yn, sg, wf, wg, wn, wo, l, *, tm):
    b_, t, d = x.shape
    tok = lambda wd: pl.BlockSpec((1, tm, wd), lambda b, i: (b, i, 0))
    weights = (wf, wg, wn, wo)
    vmem = (sum(2 * a.shape[1] * a.shape[2] for a in weights) + 4 * tm * d * 4
            + 2 * tm * (yf.shape[2] + yg.shape[2] + yn.shape[2] + 3 * d) * 2 + 6 * tm * d * 4 + (8 << 20))
    return pl.pallas_call(
        _merge_kernel,
        grid=(b_, t // tm),
        in_specs=[tok(d), pl.BlockSpec((1, 1, d), lambda b, i: (row_fn(b, 5), 0, 0)),
                  tok(yf.shape[2]), tok(yg.shape[2]), tok(yn.shape[2]), tok(3 * d)]
                 + [_layer_weight(a, l) for a in weights],
        out_specs=tok(d),
        out_shape=jax.ShapeDtypeStruct(x.shape, F32),
        compiler_params=_params(("parallel", "parallel"), vmem),
        name="merge",
    )(x, mod3, yf, yg, yn, sg, *weights)


def _rope_tables(seq_len):
    t = np.arange(seq_len)
    pairs = HEAD_DIM // 4
    freqs = jnp.asarray(ROPE_THETA, F32) ** (-jnp.arange(pairs, dtype=F32) / pairs)
    row = jnp.asarray(t // GRID_W, F32)
    col = jnp.asarray(t % GRID_W, F32)
    ang = jnp.concatenate([row[:, None] * freqs, col[:, None] * freqs], axis=-1)
    cos = jnp.repeat(jnp.cos(ang), 2, axis=-1)
    sin = jnp.repeat(jnp.sin(ang), 2, axis=-1) * jnp.asarray(np.tile([-1.0, 1.0], HEAD_DIM // 2), F32)
    reps = LANES // HEAD_DIM
    return jnp.tile(cos, (1, reps)), jnp.tile(sin, (1, reps))


def kernel(x, c, ctx, c_ctx, mod_w, mod_b, norm_ffn1, ffn1_w_in, ffn1_w_out, norm_mix, w_in, q_norm, k_norm,
           na_rpb, w_fourier, w_gqa_out, w_na_out, w_o, norm_ffn2, ffn2_w_in, ffn2_w_out, final_norm):
    bsz, s_len, d = x.shape
    c_len = ctx.shape[1]
    depth = mod_w.shape[0]
    fw, qw, kw, naw = d // 4, d // 2, d // 8, d // 4
    n_q_heads = qw // HEAD_DIM
    assert kw == LANES and n_q_heads // (kw // HEAD_DIM) == KV_GROUP
    assert s_len % (NA_Q_ROWS * GRID_W) == 0 and s_len // GRID_W >= NA_K_ROWS
    ctx_rows = 32
    assert bsz < ctx_rows

    cc = jnp.zeros((ctx_rows, d), F32).at[:bsz].set(c).at[bsz].set(c_ctx)
    mod3 = _modulation(cc, mod_w, mod_b).reshape(depth * ctx_rows * N_MOD, 1, d)

    rope = _rope_tables(s_len)
    dft_x = _dft_tables(s_len, fw)
    dft_h = _dft_tables(c_len, fw)
    ones = jnp.asarray(np.kron(np.eye(n_q_heads), np.ones((HEAD_DIM, HEAD_DIM))), BF16)
    bf = lambda a: a.astype(BF16)
    w1i, w1o, w2i, w2o, w_proj = bf(ffn1_w_in), bf(ffn1_w_out), bf(ffn2_w_in), bf(ffn2_w_out), bf(w_in)
    w_f, w_g, w_n, w_out = bf(w_fourier), bf(w_gqa_out), bf(w_na_out), bf(w_o)

    q_tiles = qw // LANES
    per_group = q_tiles // (kw // HEAD_DIM)
    gqa_x = tuple(((2 * (c // per_group), 2 * (c // per_group) + 1, 0),
                   (4 + 2 * (c // per_group), 5 + 2 * (c // per_group), 0)) for c in range(q_tiles))
    gqa_h = tuple(((2 * (c // per_group), 2 * (c // per_group) + 1, 0),) for c in range(q_tiles))
    na_h = tuple(((0, 1, c * LANES),) for c in range(naw // LANES))

    tm = 512 if s_len % 512 == 0 else 256
    tq_att = next(t for t in (2048, 1024, 256) if s_len % t == 0)
    na_bps = next(n for n in (4, 2, 1) if s_len % (n * NA_Q_ROWS * GRID_W) == 0)
    h = ctx
    hb = 4 if bsz % 4 == 0 else 1
    for l in range(depth):
        last = l == depth - 1
        row_x = lambda b, k, l=l: (l * ctx_rows + b) * N_MOD + k
        row_h = lambda b, k, l=l: (l * ctx_rows + bsz) * N_MOD + k
        qg = jnp.tile(q_norm[l], n_q_heads).reshape(1, qw)
        kg = jnp.tile(k_norm[l], kw // HEAD_DIM).reshape(1, kw)
        a_tab = _na_bias_tables(na_rpb[l], s_len // GRID_W)

        x = _ffn(x, mod3, row_x, 0, norm_ffn1[l], w1i, w1o, l, tm=tm)
        h = _ffn(h.reshape(1, bsz * c_len, d), mod3, row_h, 0, norm_ffn1[l], w1i, w1o, l,
                 tm=hb * c_len).reshape(bsz, c_len, d)

        fx, gq, ka, kb, va, vb, nq, nk, nv, sgx = _inproj(
            x, mod3, row_x, norm_mix[l], w_proj, l, qg, kg, ones, rope, nb=1, tm=tm)
        fh, gqh, kah, kbh, vah, vbh, nqh, nkh, nvh, sgh = _inproj(
            h, mod3, row_h, norm_mix[l], w_proj, l, qg, kg, ones, None, nb=hb, tm=c_len)

        yf = _fourier(fx, dft_x)
        yg = _gqa(gq, [ka, va, kb, vb, kah, vah, kbh, vbh], gqa_x, tq=tq_att, sub=256)
        yn = _na(nq, nk, nv, nkh, nvh, a_tab, blocks_per_step=na_bps)
        x = _merge(x, mod3, row_x, yf, yg, yn, sgx, w_f, w_g, w_n, w_out, l, tm=tm)
        x = _ffn(x, mod3, row_x, 6, norm_ffn2[l], w2i, w2o, l, tm=tm, final_gain=final_norm if last else None)
        if not last:
            yfh = _fourier(fh, dft_h)
            ygh = _gqa(gqh, [kah, vah, kbh, vbh], gqa_h, tq=c_len, sub=c_len)
            ynh = _gqa(nqh, [nkh, nvh], na_h, tq=c_len, sub=c_len)
            h = _merge(h, mod3, row_h, yfh, ygh, ynh, sgh, w_f, w_g, w_n, w_out, l, tm=c_len)
            h = _ffn(h.reshape(1, bsz * c_len, d), mod3, row_h, 6, norm_ffn2[l], w2i, w2o, l,
                     tm=hb * c_len).reshape(bsz, c_len, d)
    return x
```

```python
import functools
import math

import numpy as np
import jax
import jax.numpy as jnp
from jax import lax
from jax.experimental import pallas as pl
from jax.experimental.pallas import tpu as pltpu

HEAD_DIM = 64
GRID_W = 64
FOURIER_GROUPS = 4
NA_WIN_H = 8
NA_WIN_W = 16
ROPE_THETA = 10000.0
N_MOD = 9
EPS = 1e-6
KV_GROUP = 4
DFT_TABLE_SPLIT = 64

LANES = 128
VMEM_PHYSICAL_BYTES = 64 * 1024 * 1024

NA_Q_ROWS = 4
NA_K_ROWS = NA_Q_ROWS + NA_WIN_H
NEG_BIG = -1e30
LOG2_E = math.log2(math.e)
SCORE_SCALE = HEAD_DIM ** -0.5 * LOG2_E

F32 = jnp.float32
BF16 = jnp.bfloat16


def _dot(a, b):
    return jnp.dot(a, b, preferred_element_type=F32)


def _dot_nt(a, b):
    return lax.dot_general(a, b, (((1,), (1,)), ((), ())), preferred_element_type=F32)


def _params(sem, vmem_bytes):
    return pltpu.CompilerParams(dimension_semantics=sem,
                                vmem_limit_bytes=int(min(vmem_bytes, VMEM_PHYSICAL_BYTES - (4 << 20))))


def _resident(shape, index_map):
    return pl.BlockSpec(shape, index_map, pipeline_mode=pl.Buffered(1))


def _layer_weight(w, l):
    return _resident((None,) + w.shape[1:], lambda *_: (l, 0, 0))


def _norm_mod(x, gain, shift, scale):
    y = x * lax.rsqrt(jnp.mean(x * x, axis=-1, keepdims=True) + EPS)
    return y * (gain * (1.0 + scale)) + shift


def _mod_kernel(c_ref, w_ref, b_ref, o_ref):
    c = c_ref[...]
    s = (c * jax.nn.sigmoid(c)).astype(BF16)
    o_ref[0] = _dot(s, w_ref[0].astype(BF16)) + b_ref[0]


def _modulation(cc, mod_w, mod_b, tn=1024):
    depth, d, n = mod_w.shape
    rows = cc.shape[0]
    return pl.pallas_call(
        _mod_kernel,
        grid=(depth, n // tn),
        in_specs=[pl.BlockSpec((rows, d), lambda l, j: (0, 0)),
                  pl.BlockSpec((1, d, tn), lambda l, j: (l, 0, j)),
                  pl.BlockSpec((1, 1, tn), lambda l, j: (l, 0, j))],
        out_specs=pl.BlockSpec((1, rows, tn), lambda l, j: (l, 0, j)),
        out_shape=jax.ShapeDtypeStruct((depth, rows, n), F32),
        compiler_params=_params(("parallel", "parallel"), 2 * (d * tn * 4) + 8 * d * tn),
        name="modulation",
    )(cc, mod_w, mod_b.reshape(depth, 1, n))


def _ffn_kernel(*refs, d_ff, tf, final):
    if final:
        x_ref, sh_ref, sc_ref, gt_ref, g_ref, win_ref, wout_ref, fg_ref, o_ref, u_scr, h_scr = refs
    else:
        x_ref, sh_ref, sc_ref, gt_ref, g_ref, win_ref, wout_ref, o_ref, u_scr, h_scr = refs
    x = x_ref[0]
    u_scr[...] = _norm_mod(x, g_ref[...], sh_ref[0], sc_ref[0]).astype(BF16)
    for j in range(d_ff // tf):
        u = u_scr[...]
        a = _dot(u, win_ref[:, j * tf:(j + 1) * tf])
        g = _dot(u, win_ref[:, d_ff + j * tf:d_ff + (j + 1) * tf])
        h_scr[:, j * tf:(j + 1) * tf] = (g * jax.nn.sigmoid(g) * a).astype(BF16)
    y = x + (0.5 * gt_ref[0]) * _dot(h_scr[...], wout_ref[...])
    if final:
        y = y * lax.rsqrt(jnp.mean(y * y, axis=-1, keepdims=True) + EPS) * fg_ref[...]
    o_ref[0] = y


def _ffn(x, mod3, row_fn, k0, gain, w_in, w_out, l, *, tm, final_gain=None, tf=256):
    nb, t, d = x.shape
    d_ff = w_out.shape[1]
    final = final_gain is not None
    row = lambda k: pl.BlockSpec((1, 1, d), lambda b, i: (row_fn(b, k), 0, 0))
    vec = pl.BlockSpec((1, d), lambda b, i: (0, 0))
    in_specs = [pl.BlockSpec((1, tm, d), lambda b, i: (b, i, 0)), row(k0), row(k0 + 1), row(k0 + 2), vec,
                _layer_weight(w_in, l), _layer_weight(w_out, l)]
    args = [x, mod3, mod3, mod3, gain.reshape(1, d), w_in, w_out]
    if final:
        in_specs.append(vec)
        args.append(final_gain.reshape(1, d))
    vmem = (3 * d * d_ff * 2
            + 4 * tm * d * 4
            + tm * d * 2 + tm * d_ff * 2
            + 2 * tm * d * 4 + 4 * tm * tf * 4)
    return pl.pallas_call(
        functools.partial(_ffn_kernel, d_ff=d_ff, tf=tf, final=final),
        grid=(nb, t // tm),
        in_specs=in_specs,
        out_specs=pl.BlockSpec((1, tm, d), lambda b, i: (b, i, 0)),
        out_shape=jax.ShapeDtypeStruct(x.shape, F32),
        scratch_shapes=[pltpu.VMEM((tm, d), BF16), pltpu.VMEM((tm, d_ff), BF16)],
        compiler_params=_params(("parallel", "parallel"), vmem + (8 << 20)),
        name="ffn",
    )(*args)


def _head_norm(x, gain, ones):
    ss = _dot((x * x).astype(BF16), ones)
    return x * lax.rsqrt(ss * (1.0 / HEAD_DIM) + EPS) * gain


def _rope(y, cos, ssin):
    even = lax.broadcasted_iota(jnp.int32, (1, LANES), 1) % 2 == 0
    cols = []
    for c in range(y.shape[1] // LANES):
        yc = y[:, c * LANES:(c + 1) * LANES]
        partner = jnp.where(even, pltpu.roll(yc, LANES - 1, 1), pltpu.roll(yc, 1, 1))
        cols.append(yc * cos + partner * ssin)
    return cols[0] if len(cols) == 1 else jnp.concatenate(cols, axis=1)


def _inproj_kernel(*refs, rope, splits, d):
    if rope:
        (x_ref, sh_ref, sc_ref, g_ref, w_ref, qg_ref, kg_ref, ones_ref, cos_ref, sin_ref,
         f_ref, q_ref, ka_ref, kb_ref, va_ref, vb_ref, nq_ref, nk_ref, nv_ref, sg_ref, u_scr) = refs
    else:
        (x_ref, sh_ref, sc_ref, g_ref, w_ref, qg_ref, kg_ref, ones_ref,
         f_ref, q_ref, ka_ref, kb_ref, va_ref, vb_ref, nq_ref, nk_ref, nv_ref, sg_ref, u_scr) = refs
    m = u_scr.shape[0]
    x = x_ref[...].reshape(m, d)
    u_scr[...] = _norm_mod(x, g_ref[...], sh_ref[0], sc_ref[0]).astype(BF16)
    s_f, s_q, s_k, s_v, s_nq, s_nk, s_nv = splits
    r = _dot(u_scr[...], w_ref[:, :s_nv])
    kw = s_k - s_q
    q = _head_norm(r[:, s_f:s_q], qg_ref[...], ones_ref[...])
    k = _head_norm(r[:, s_q:s_k], kg_ref[...], ones_ref[:kw, :kw])
    if rope:
        q = _rope(q, cos_ref[...], sin_ref[...])
        k = _rope(k, cos_ref[...], sin_ref[...])
    scale = SCORE_SCALE
    put = lambda ref, val: ref.__setitem__(Ellipsis, val.astype(BF16).reshape(ref.shape))
    put(f_ref, r[:, :s_f])
    put(q_ref, q * scale)
    lo = _lane_lo()
    for val, a_ref, b_ref in ((k, ka_ref, kb_ref), (r[:, s_k:s_v], va_ref, vb_ref)):
        swapped = pltpu.roll(val, HEAD_DIM, 1)
        put(a_ref, jnp.where(lo, val, swapped))
        put(b_ref, jnp.where(lo, swapped, val))
    put(nq_ref, r[:, s_v:s_nq] * scale)
    put(nk_ref, r[:, s_nq:s_nk])
    put(nv_ref, r[:, s_nk:s_nv])
    for c in range(3):
        g = _dot(u_scr[...], w_ref[:, s_nv + c * d:s_nv + (c + 1) * d])
        sg_ref[..., c * d:(c + 1) * d] = jax.nn.sigmoid(g).astype(BF16).reshape(sg_ref.shape[:-1] + (d,))


def _inproj(x, mod3, row_fn, gain, w, l, qg, kg, ones, tables, *, nb, tm):
    b_, t, d = x.shape
    n_tot = w.shape[2]
    fw, qw, kw, naw = d // 4, d // 2, d // 8, d // 4
    widths = [fw, qw, kw, kw, naw, naw, naw]
    splits = tuple(int(v) for v in np.cumsum(widths))
    rope = tables is not None
    row = lambda k: pl.BlockSpec((1, 1, d), lambda b, i: (row_fn(b, k), 0, 0))
    const = lambda shape: pl.BlockSpec(shape, lambda b, i: (0,) * len(shape))
    tok = lambda wd: pl.BlockSpec((nb, tm, wd), lambda b, i: (b, i, 0))
    in_specs = [tok(d), row(3), row(4), const((1, d)), _layer_weight(w, l),
                const((1, qw)), const((1, kw)), const((qw, qw))]
    args = [x, mod3, mod3, gain.reshape(1, d), w, qg, kg, ones]
    if rope:
        in_specs += [pl.BlockSpec((tm, LANES), lambda b, i: (i, 0))] * 2
        args += list(tables)
    out_widths = [fw, qw, kw, kw, kw, kw, naw, naw, naw, 3 * d]
    m = nb * tm
    vmem = (d * n_tot * 2 + 2 * m * d * 4 + m * d * 2 + 2 * m * n_tot * 2
            + m * splits[-1] * 4 + 6 * m * qw * 4 + 2 * m * d * 4)
    return pl.pallas_call(
        functools.partial(_inproj_kernel, rope=rope, splits=splits, d=d),
        grid=(b_ // nb, t // tm),
        in_specs=in_specs,
        out_specs=[tok(wd) for wd in out_widths],
        out_shape=[jax.ShapeDtypeStruct((b_, t, wd), BF16) for wd in out_widths],
        scratch_shapes=[pltpu.VMEM((m, d), BF16)],
        compiler_params=_params(("parallel", "parallel"), vmem + (8 << 20)),
        name="inproj",
    )(*args)


def _fourier_kernel(x_ref, cs_ref, ct_ref, st_ref, o_ref):
    w = x_ref.shape[2]
    ab = _dot(x_ref[0], cs_ref[...]).astype(BF16)
    y = _dot(ct_ref[...], ab[:, :w]) - _dot(st_ref[...], ab[:, w:])
    o_ref[0] = y.astype(BF16)


def _dft_tables(t, w):
    def angles(rows, cols, n):
        prod = (jnp.arange(rows, dtype=jnp.int32)[:, None] * jnp.arange(cols, dtype=jnp.int32)[None, :]) % n
        return prod.astype(F32) * (2.0 * math.pi / n)

    def cs(n):
        ang = angles(n, n, n)
        return jnp.cos(ang) * (1.0 / math.sqrt(n)), jnp.sin(ang) * (1.0 / math.sqrt(n))

    def cs_split(n, outer):
        inner = n // outer
        ang_a = angles(outer, n, outer)[:, None, :]
        ang_b = angles(inner, n, n)[None, :, :]
        ca, sa = jnp.cos(ang_a), jnp.sin(ang_a)
        cb, sb = jnp.cos(ang_b) * (1.0 / math.sqrt(n)), jnp.sin(ang_b) * (1.0 / math.sqrt(n))
        return (ca * cb - sa * sb).reshape(n, n), (sa * cb + ca * sb).reshape(n, n)

    ct, st = cs_split(t, DFT_TABLE_SPLIT) if t % DFT_TABLE_SPLIT == 0 and t > DFT_TABLE_SPLIT else cs(t)
    cg, sg = cs(w // FOURIER_GROUPS)
    eye = jnp.eye(FOURIER_GROUPS, dtype=F32)
    cs_c = jnp.concatenate([jnp.kron(eye, cg), jnp.kron(eye, sg)], axis=1)
    return cs_c.astype(BF16), ct.astype(BF16), st.astype(BF16)


def _fourier(f, tables):
    b_, t, w = f.shape
    cs_c, ct, st = tables
    return pl.pallas_call(
        _fourier_kernel,
        grid=(b_,),
        in_specs=[pl.BlockSpec((1, t, w), lambda b: (b, 0, 0)),
                  _resident((w, 2 * w), lambda b: (0, 0)),
                  _resident((t, t), lambda b: (0, 0)),
                  _resident((t, t), lambda b: (0, 0))],
        out_specs=pl.BlockSpec((1, t, w), lambda b: (b, 0, 0)),
        out_shape=jax.ShapeDtypeStruct(f.shape, BF16),
        compiler_params=_params(("parallel",), 2 * t * t * 2 + 4 * t * w * 2 + 4 * t * w * 4 + (8 << 20)),
        name="fourier",
    )(f, cs_c, ct, st)


def _lane_lo():
    return lax.broadcasted_iota(jnp.int32, (1, LANES), 1) < HEAD_DIM


def _stack_heads(qc, lo):
    zero = jnp.zeros_like(qc)
    return jnp.concatenate([jnp.where(lo, qc, zero), jnp.where(lo, zero, qc)], axis=0)


def _unstack_heads(o, lo):
    t = o.shape[0] // 2
    return jnp.where(lo, o[:t], o[t:])


def _attend(q2, blocks):
    scores = []
    for k, _, bias in blocks:
        s = _dot_nt(q2, k)
        scores.append(s if bias is None else s + bias)
    m = functools.reduce(jnp.maximum, [jnp.max(s, axis=-1, keepdims=True) for s in scores])
    o = functools.reduce(jnp.add, [_dot(jnp.exp2(s - m).astype(BF16), v) for s, (_, v, _) in zip(scores, blocks)])
    return o[:, :LANES] / o[:, LANES:]


def _with_ones(v):
    return jnp.concatenate([v, jnp.ones_like(v)], axis=1)


def _gqa_kernel(*refs, sub, tile_kv):
    q_ref, o_ref = refs[0], refs[-1]
    kv_refs = refs[1:-1]
    lo = _lane_lo()
    slabs = {}
    for entry in sorted({e for tile in tile_kv for e in tile}):
        ki, vi, col0 = entry
        cols = slice(col0, col0 + LANES)
        slabs[entry] = (kv_refs[ki][0, :, cols], _with_ones(kv_refs[vi][0, :, cols]), None)
    for r in range(q_ref.shape[1] // sub):
        rows = slice(r * sub, (r + 1) * sub)
        for c, tile in enumerate(tile_kv):
            cols = slice(c * LANES, (c + 1) * LANES)
            q2 = _stack_heads(q_ref[0, rows, cols], lo)
            o_ref[0, rows, cols] = _unstack_heads(_attend(q2, [slabs[e] for e in tile]), lo).astype(BF16)


def _gqa(q, kv_arrays, tile_kv, *, tq, sub):
    b_, t, w = q.shape
    assert len(tile_kv) == w // LANES
    in_specs = [pl.BlockSpec((1, tq, w), lambda b, i: (b, i, 0))]
    in_specs += [pl.BlockSpec((1,) + a.shape[1:], lambda b, i: (b, 0, 0)) for a in kv_arrays]
    args = [q] + list(kv_arrays)
    n_keys = max(sum(kv_arrays[ki].shape[1] for ki, _, _ in tile) for tile in tile_kv)
    vmem = (4 * tq * w * 2 + sum(2 * a.shape[1] * a.shape[2] * 2 for a in kv_arrays)
            + 3 * (2 * sub) * n_keys * (4 + 4 + 2) + (8 << 20))
    return pl.pallas_call(
        functools.partial(_gqa_kernel, sub=sub, tile_kv=tile_kv),
        grid=(b_, t // tq),
        in_specs=in_specs,
        out_specs=pl.BlockSpec((1, tq, w), lambda b, i: (b, i, 0)),
        out_shape=jax.ShapeDtypeStruct(q.shape, BF16),
        compiler_params=_params(("parallel", "parallel"), vmem),
        name="gqa",
    )(*args)


def _na_kernel(q_ref, k_ref, v_ref, kc_ref, vc_ref, a_ref, o_ref, *, rows):
    tq = NA_Q_ROWS * GRID_W
    per_step = q_ref.shape[1] // tq
    nblk = rows // NA_Q_ROWS
    lo = _lane_lo()
    for r in range(per_step):
        j = pl.program_id(1) * per_step + r
        sel = jnp.where(j == 0, 0, jnp.where(j == nblk - 1, 2, 1))
        ks = jnp.clip(j * NA_Q_ROWS - NA_WIN_H // 2, 0, rows - NA_K_ROWS)
        loc = pl.ds(pl.multiple_of(ks * GRID_W, NA_Q_ROWS * GRID_W), NA_K_ROWS * GRID_W)
        q_rows = slice(r * tq, (r + 1) * tq)
        for c in range(q_ref.shape[2] // LANES):
            cols = slice(c * LANES, (c + 1) * LANES)
            q2 = _stack_heads(q_ref[0, q_rows, cols], lo)
            blocks = [(k_ref[0, loc, cols], _with_ones(v_ref[0, loc, cols]), a_ref[sel, c]),
                      (kc_ref[0, :, cols], _with_ones(vc_ref[0, :, cols]), None)]
            o_ref[0, q_rows, cols] = _unstack_heads(_attend(q2, blocks), lo).astype(BF16)


def _na_bias_tables(rpb, rows):
    nh, n_dr, n_dc = rpb.shape
    tq = NA_Q_ROWS * GRID_W
    return pl.pallas_call(
        functools.partial(_na_bias_kernel, rows=rows, nh=nh, n_dr=n_dr, n_dc=n_dc),
        in_specs=[pl.BlockSpec(memory_space=pltpu.SMEM)],
        out_shape=jax.ShapeDtypeStruct((3, nh // 2, 2 * tq, NA_K_ROWS * GRID_W), F32),
        compiler_params=pltpu.CompilerParams(
            vmem_limit_bytes=2 * 3 * nh * tq * NA_K_ROWS * GRID_W * 4 + (8 << 20)),
        name="na_bias",
    )(rpb.reshape(-1))


def _na_bias_kernel(rpb_ref, o_ref, *, rows, nh, n_dr, n_dc):
    kh = min(NA_WIN_H, rows)
    qc = lax.broadcasted_iota(jnp.int32, (GRID_W, GRID_W), 0)
    kc = lax.broadcasted_iota(jnp.int32, (GRID_W, GRID_W), 1)
    cs = jnp.clip(qc - NA_WIN_W // 2, 0, GRID_W - NA_WIN_W)
    band = (kc >= cs) & (kc < cs + NA_WIN_W)
    dc = kc - qc + (NA_WIN_W - 1)
    o_ref[...] = jnp.full(o_ref.shape, NEG_BIG, F32)
    uses = {}
    for sel, r0 in enumerate((0, NA_Q_ROWS, rows - NA_Q_ROWS)):
        ks = int(np.clip(r0 - NA_WIN_H // 2, 0, rows - NA_K_ROWS))
        for qi in range(NA_Q_ROWS):
            r = r0 + qi
            rs = int(np.clip(r - kh // 2, 0, rows - kh))
            for ki in range(NA_K_ROWS):
                kr = ks + ki
                if rs <= kr < rs + kh:
                    uses.setdefault(kr - r + NA_WIN_H - 1, []).append((sel, qi, ki))
    for h in range(nh):
        c, e = divmod(h, 2)
        for dr, dests in sorted(uses.items()):
            tile = jnp.full((GRID_W, GRID_W), NEG_BIG, F32)
            for j in range(n_dc):
                tile = jnp.where(band & (dc == j), rpb_ref[(h * n_dr + dr) * n_dc + j] * LOG2_E, tile)
            for sel, qi, ki in dests:
                r_lo = (e * NA_Q_ROWS + qi) * GRID_W
                o_ref[sel, c, r_lo:r_lo + GRID_W, ki * GRID_W:(ki + 1) * GRID_W] = tile


def _na(q, k, v, kc, vc, a_tab, *, blocks_per_step):
    b_, t, w = q.shape
    blk = NA_Q_ROWS * GRID_W
    tq = blocks_per_step * blk
    n_loc = NA_K_ROWS * GRID_W
    full = lambda a: pl.BlockSpec((1,) + a.shape[1:], lambda b, i: (b, 0, 0))
    vmem = (a_tab.size * 4 + 4 * tq * w * 2 + 4 * (k.shape[1] + kc.shape[1]) * w * 2
            + 3 * (2 * blk) * (n_loc + kc.shape[1]) * (4 + 4 + 2) + (8 << 20))
    return pl.pallas_call(
        functools.partial(_na_kernel, rows=t // GRID_W),
        grid=(b_, t // tq),
        in_specs=[pl.BlockSpec((1, tq, w), lambda b, i: (b, i, 0)), full(k), full(v), full(kc), full(vc),
                  _resident(a_tab.shape, lambda b, i: (0, 0, 0, 0))],
        out_specs=pl.BlockSpec((1, tq, w), lambda b, i: (b, i, 0)),
        out_shape=jax.ShapeDtypeStruct(q.shape, BF16),
        compiler_params=_params(("parallel", "arbitrary"), vmem),
        name="na",
    )(q, k, v, kc, vc, a_tab)


def _merge_kernel(x_ref, gt_ref, yf_ref, yg_ref, yn_ref, sg_ref, wf_ref, wg_ref, wn_ref, wo_ref, o_ref):
    d = x_ref.shape[2]
    m = (sg_ref[0, :, :d].astype(F32) * _dot(yf_ref[0], wf_ref[...])
         + sg_ref[0, :, d:2 * d].astype(F32) * _dot(yg_ref[0], wg_ref[...])
         + sg_ref[0, :, 2 * d:].astype(F32) * _dot(yn_ref[0], wn_ref[...]))
    o_ref[0] = x_ref[0] + gt_ref[0] * _dot(m.astype(BF16), wo_ref[...])


def _merge(x, mod3, row_fn, yf, yg, yn, sg, wf, wg, wn, wo, l, *, tm):
    b_, t, d = x.shape
    tok = lambda wd: pl.BlockSpec((1, tm, wd), lambda b, i: (b, i, 0))
    weights = (wf, wg, wn, wo)
    vmem = (sum(2 * a.shape[1] * a.shape[2] for a in weights) + 4 * tm * d * 4
            + 2 * tm * (yf.shape[2] + yg.shape[2] + yn.shape[2] + 3 * d) * 2 + 6 * tm * d * 4 + (8 << 20))
    return pl.pallas_call(
        _merge_kernel,
        grid=(b_, t // tm),
        in_specs=[tok(d), pl.BlockSpec((1, 1, d), lambda b, i: (row_fn(b, 5), 0, 0)),
                  tok(yf.shape[2]), tok(yg.shape[2]), tok(yn.shape[2]), tok(3 * d)]
                 + [_layer_weight(a, l) for a in weights],
        out_specs=tok(d),
        out_shape=jax.ShapeDtypeStruct(x.shape, F32),
        compiler_params=_params(("parallel", "parallel"), vmem),
        name="merge",
    )(x, mod3, yf, yg, yn, sg, *weights)


def _rope_tables(seq_len):
    t = np.arange(seq_len)
    pairs = HEAD_DIM // 4
    freqs = jnp.asarray(ROPE_THETA, F32) ** (-jnp.arange(pairs, dtype=F32) / pairs)
    row = jnp.asarray(t // GRID_W, F32)
    col = jnp.asarray(t % GRID_W, F32)
    ang = jnp.concatenate([row[:, None] * freqs, col[:, None] * freqs], axis=-1)
    cos = jnp.repeat(jnp.cos(ang), 2, axis=-1)
    sin = jnp.repeat(jnp.sin(ang), 2, axis=-1) * jnp.asarray(np.tile([-1.0, 1.0], HEAD_DIM // 2), F32)
    reps = LANES // HEAD_DIM
    return jnp.tile(cos, (1, reps)), jnp.tile(sin, (1, reps))


def kernel(x, c, ctx, c_ctx, mod_w, mod_b, norm_ffn1, ffn1_w_in, ffn1_w_out, norm_mix, w_in, q_norm, k_norm,
           na_rpb, w_fourier, w_gqa_out, w_na_out, w_o, norm_ffn2, ffn2_w_in, ffn2_w_out, final_norm):
    bsz, s_len, d = x.shape
    c_len = ctx.shape[1]
    depth = mod_w.shape[0]
    fw, qw, kw, naw = d // 4, d // 2, d // 8, d // 4
    n_q_heads = qw // HEAD_DIM
    assert kw == LANES and n_q_heads // (kw // HEAD_DIM) == KV_GROUP
    assert s_len % (NA_Q_ROWS * GRID_W) == 0 and s_len // GRID_W >= NA_K_ROWS
    ctx_rows = 32
    assert bsz < ctx_rows

    cc = jnp.zeros((ctx_rows, d), F32).at[:bsz].set(c).at[bsz].set(c_ctx)
    mod3 = _modulation(cc, mod_w, mod_b).reshape(depth * ctx_rows * N_MOD, 1, d)

    rope = _rope_tables(s_len)
    dft_x = _dft_tables(s_len, fw)
    dft_h = _dft_tables(c_len, fw)
    ones = jnp.asarray(np.kron(np.eye(n_q_heads), np.ones((HEAD_DIM, HEAD_DIM))), BF16)
    bf = lambda a: a.astype(BF16)
    w1i, w1o, w2i, w2o, w_proj = bf(ffn1_w_in), bf(ffn1_w_out), bf(ffn2_w_in), bf(ffn2_w_out), bf(w_in)
    w_f, w_g, w_n, w_out = bf(w_fourier), bf(w_gqa_out), bf(w_na_out), bf(w_o)

    q_tiles = qw // LANES
    per_group = q_tiles // (kw // HEAD_DIM)
    gqa_x = tuple(((2 * (c // per_group), 2 * (c // per_group) + 1, 0),
                   (4 + 2 * (c // per_group), 5 + 2 * (c // per_group), 0)) for c in range(q_tiles))
    gqa_h = tuple(((2 * (c // per_group), 2 * (c // per_group) + 1, 0),) for c in range(q_tiles))
    na_h = tuple(((0, 1, c * LANES),) for c in range(naw // LANES))

    tm = 512 if s_len % 512 == 0 else 256
    tq_att = next(t for t in (1024, 256) if s_len % t == 0)
    na_bps = next(n for n in (4, 2, 1) if s_len % (n * NA_Q_ROWS * GRID_W) == 0)
    h = ctx
    hb = 4 if bsz % 4 == 0 else 1
    for l in range(depth):
        last = l == depth - 1
        row_x = lambda b, k, l=l: (l * ctx_rows + b) * N_MOD + k
        row_h = lambda b, k, l=l: (l * ctx_rows + bsz) * N_MOD + k
        qg = jnp.tile(q_norm[l], n_q_heads).reshape(1, qw)
        kg = jnp.tile(k_norm[l], kw // HEAD_DIM).reshape(1, kw)
        a_tab = _na_bias_tables(na_rpb[l], s_len // GRID_W)

        x = _ffn(x, mod3, row_x, 0, norm_ffn1[l], w1i, w1o, l, tm=tm)
        h = _ffn(h.reshape(1, bsz * c_len, d), mod3, row_h, 0, norm_ffn1[l], w1i, w1o, l,
                 tm=hb * c_len).reshape(bsz, c_len, d)

        fx, gq, ka, kb, va, vb, nq, nk, nv, sgx = _inproj(
            x, mod3, row_x, norm_mix[l], w_proj, l, qg, kg, ones, rope, nb=1, tm=tm)
        fh, gqh, kah, kbh, vah, vbh, nqh, nkh, nvh, sgh = _inproj(
            h, mod3, row_h, norm_mix[l], w_proj, l, qg, kg, ones, None, nb=hb, tm=c_len)

        yf = _fourier(fx, dft_x)
        yg = _gqa(gq, [ka, va, kb, vb, kah, vah, kbh, vbh], gqa_x, tq=tq_att, sub=256)
        yn = _na(nq, nk, nv, nkh, nvh, a_tab, blocks_per_step=na_bps)
        x = _merge(x, mod3, row_x, yf, yg, yn, sgx, w_f, w_g, w_n, w_out, l, tm=tm)
        x = _ffn(x, mod3, row_x, 6, norm_ffn2[l], w2i, w2o, l, tm=tm, final_gain=final_norm if last else None)
        if not last:
            yfh = _fourier(fh, dft_h)
            ygh = _gqa(gqh, [kah, vah, kbh, vbh], gqa_h, tq=c_len, sub=c_len)
            ynh = _gqa(nqh, [nkh, nvh], na_h, tq=c_len, sub=c_len)
            h = _merge(h, mod3, row_h, yfh, ygh, ynh, sgh, w_f, w_g, w_n, w_out, l, tm=c_len)
            h = _ffn(h.reshape(1, bsz * c_len, d), mod3, row_h, 6, norm_ffn2[l], w2i, w2o, l,
                     tm=hb * c_len).reshape(bsz, c_len, d)
    return x
```
